```python
import jax, jax.numpy as jnp
from jax import lax
import numpy as np

D_MODEL = 1024
BATCH = 8
SEQ = 4096
DEPTH = 1

CHUNK = 64
SB_BLOCK = 128
PLE_DIM = 256
MIX_WIDTH = D_MODEL
SB_WIDTH = MIX_WIDTH // 2
ML_WIDTH = MIX_WIDTH - SB_WIDTH
SB_HEADS = 8
SB_HEAD_DIM = SB_WIDTH // SB_HEADS
ML_HEADS = 4
ML_HEAD_DIM = ML_WIDTH // ML_HEADS
CONV_K = 4
EPS = 1e-6
SPLIT_SIZES = (SB_WIDTH, SB_WIDTH, SB_WIDTH, SB_WIDTH, ML_WIDTH, ML_WIDTH, ML_WIDTH, ML_WIDTH, ML_WIDTH, ML_HEADS, ML_HEADS)
N_IN = 4 * SB_WIDTH + 5 * ML_WIDTH + 2 * ML_HEADS

kernel_name = 'hybrid_stickbreaking_mlstm_block'


def _rmsnorm(x, w):
    xf = x.astype(jnp.float32)
    y = xf * lax.rsqrt(jnp.mean(xf * xf, axis=-1, keepdims=True) + EPS)
    return (y * w.astype(jnp.float32)).astype(x.dtype)


def _head_rmsnorm(y, w, n_heads):
    b, s, width = y.shape
    yf = y.astype(jnp.float32).reshape(b, s, n_heads, width // n_heads)
    yf = yf * lax.rsqrt(jnp.mean(yf * yf, axis=-1, keepdims=True) + EPS)
    return (yf.reshape(b, s, width) * w.astype(jnp.float32)).astype(y.dtype)


def _split_cols(u):
    bounds = []
    acc = 0
    for size in SPLIT_SIZES[:-1]:
        acc += size
        bounds.append(acc)
    return jnp.split(u, bounds, axis=-1)


def _split_heads(u, n_heads):
    b, s, width = u.shape
    return u.reshape(b, s, n_heads, width // n_heads).transpose(0, 2, 1, 3)


def _merge_heads(u):
    b, h, s, d = u.shape
    return u.transpose(0, 2, 1, 3).reshape(b, s, h * d)


def _causal_conv(u, w, bias):
    s = u.shape[1]
    k = w.shape[0]
    up = jnp.pad(u, ((0, 0), (k - 1, 0), (0, 0)))
    out = bias
    for j in range(k):
        out = out + w[j] * up[:, j:j + s]
    return out


def _stick_breaking_attention(q, k, v):
    s = q.shape[2]
    d = q.shape[3]
    scale = d ** -0.5
    outs = []
    for blk in range(s // SB_BLOCK):
        q0 = blk * SB_BLOCK
        k_end = q0 + SB_BLOCK
        qb = q[:, :, q0:k_end]
        kb = k[:, :, :k_end]
        vb = v[:, :, :k_end]
        z = jnp.einsum('bhqd,bhkd->bhqk', qb, kb).astype(jnp.float32) * scale
        t_pos = q0 + jnp.arange(SB_BLOCK)[:, None]
        s_pos = jnp.arange(k_end)[None, :]
        strict = s_pos < t_pos
        log_beta = jax.nn.log_sigmoid(z)
        log_keep = jnp.where(strict, log_beta - z, 0.0)
        after = lax.cumsum(log_keep, axis=3, reverse=True) - log_keep
        a = jnp.where(strict, jnp.exp(log_beta + after), 0.0)
        outs.append(jnp.einsum('bhqk,bhkd->bhqd', a, vb.astype(jnp.float32)))
    return jnp.concatenate(outs, axis=2).astype(q.dtype)


def _mlstm_chunkwise(q, k, v, i_pre, f_pre):
    out_dtype = q.dtype
    f32 = jnp.float32
    bsz, nh, s, d = q.shape
    nc = s // CHUNK

    def chunks(t):
        return t.astype(f32).reshape(bsz, nh, nc, CHUNK, d)

    q = chunks(q)
    k = chunks(k) * (d ** -0.5)
    v = chunks(v)
    ig = i_pre.astype(f32).transpose(0, 2, 1).reshape(bsz, nh, nc, CHUNK)
    lf = jax.nn.log_sigmoid(f_pre.astype(f32)).transpose(0, 2, 1).reshape(bsz, nh, nc, CHUNK)
    b = jnp.cumsum(lf, axis=-1)
    b_last = b[..., -1]

    g = b_last[..., None] - b + ig
    m_loc = jnp.max(g, axis=-1)
    w_loc = jnp.exp(g - m_loc[..., None])
    c_loc = jnp.einsum('bhcs,bhcsv,bhcsd->bhcvd', w_loc, v, k)
    n_loc = jnp.einsum('bhcs,bhcsd->bhcd', w_loc, k)

    def step(carry, xs):
        c_st, n_st, m_st = carry
        bl, ml, cl, nl = xs
        m_new = jnp.maximum(bl + m_st, ml)
        a = jnp.exp(bl + m_st - m_new)
        gg = jnp.exp(ml - m_new)
        c_new = a[..., None, None] * c_st + gg[..., None, None] * cl
        n_new = a[..., None] * n_st + gg[..., None] * nl
        return (c_new, n_new, m_new), (c_st, n_st, m_st)

    init = (jnp.zeros((bsz, nh, d, d), f32), jnp.zeros((bsz, nh, d), f32), jnp.zeros((bsz, nh), f32))
    xs = (jnp.moveaxis(b_last, 2, 0), jnp.moveaxis(m_loc, 2, 0), jnp.moveaxis(c_loc, 2, 0), jnp.moveaxis(n_loc, 2, 0))
    _, (c_prev, n_prev, m_prev) = lax.scan(step, init, xs)
    c_prev = jnp.moveaxis(c_prev, 0, 2)
    n_prev = jnp.moveaxis(n_prev, 0, 2)
    m_prev = jnp.moveaxis(m_prev, 0, 2)

    causal = jnp.tril(jnp.ones((CHUNK, CHUNK), dtype=bool))
    d_log = jnp.where(causal, b[..., :, None] - b[..., None, :] + ig[..., None, :], -jnp.inf)
    m_t = jnp.maximum(b + m_prev[..., None], jnp.max(d_log, axis=-1))
    w_intra = jnp.exp(d_log - m_t[..., None])
    scores = jnp.einsum('bhcld,bhcsd->bhcls', q, k) * w_intra
    carry_scale = jnp.exp(b + m_prev[..., None] - m_t)
    num = jnp.einsum('bhcls,bhcsv->bhclv', scores, v) + carry_scale[..., None] * jnp.einsum('bhcld,bhcvd->bhclv', q, c_prev)
    den = jnp.sum(scores, axis=-1) + carry_scale * jnp.einsum('bhcld,bhcd->bhcl', q, n_prev)
    h = num / jnp.maximum(jnp.abs(den), jnp.exp(-m_t))[..., None]
    return h.reshape(bsz, nh, s, d).astype(out_dtype)


def setup_inputs(seed: int = 0) -> dict:
    key = jax.random.key(seed)
    ks = jax.random.split(key, 15)
    nrm = jax.random.normal
    f32 = jnp.float32
    return {
        'x': nrm(ks[0], (BATCH, SEQ, D_MODEL), f32),
        'p': nrm(ks[1], (DEPTH, BATCH, SEQ, PLE_DIM), f32),
        'pre_norm_w': 1.0 + 0.05 * nrm(ks[2], (DEPTH, D_MODEL), f32),
        'w_in': nrm(ks[3], (DEPTH, D_MODEL, N_IN), f32) * D_MODEL ** -0.5,
        'ml_conv_w': nrm(ks[4], (DEPTH, CONV_K, 2 * ML_WIDTH), f32) * CONV_K ** -0.5,
        'ml_conv_b': 0.02 * nrm(ks[5], (DEPTH, 2 * ML_WIDTH), f32),
        'ml_i_bias': 0.1 * nrm(ks[6], (DEPTH, ML_HEADS), f32),
        'ml_f_bias': jnp.linspace(3.0, 6.0, ML_HEADS, dtype=f32)[None, :] + 0.01 * nrm(ks[7], (DEPTH, ML_HEADS), f32),
        'sb_norm_w': 1.0 + 0.05 * nrm(ks[8], (DEPTH, SB_WIDTH), f32),
        'ml_norm_w': 1.0 + 0.05 * nrm(ks[9], (DEPTH, ML_WIDTH), f32),
        'w_out': nrm(ks[10], (DEPTH, MIX_WIDTH, D_MODEL), f32) * MIX_WIDTH ** -0.5,
        'post_norm_w': 1.0 + 0.05 * nrm(ks[11], (DEPTH, D_MODEL), f32),
        'ple_w_up': nrm(ks[12], (DEPTH, PLE_DIM, D_MODEL), f32) * PLE_DIM ** -0.5,
        'ple_w_gate': nrm(ks[13], (DEPTH, D_MODEL, D_MODEL), f32) * D_MODEL ** -0.5,
        'ple_b_gate': 0.02 * nrm(ks[14], (DEPTH, D_MODEL), f32),
    }


def reference(x, p, pre_norm_w, w_in, ml_conv_w, ml_conv_b, ml_i_bias, ml_f_bias, sb_norm_w, ml_norm_w, w_out, post_norm_w, ple_w_up, ple_w_gate, ple_b_gate):
    h = x
    for i in range(DEPTH):
        u = _rmsnorm(h, pre_norm_w[i])
        proj = u @ w_in[i]
        sb_q, sb_k, sb_v, sb_z, ml_q, ml_k, ml_v, ml_o, ml_z, ml_i, ml_f = _split_cols(proj)

        y_sb = _stick_breaking_attention(_split_heads(sb_q, SB_HEADS), _split_heads(sb_k, SB_HEADS), _split_heads(sb_v, SB_HEADS))
        y_sb = _head_rmsnorm(_merge_heads(y_sb), sb_norm_w[i], SB_HEADS) * jax.nn.silu(sb_z)

        qk = jax.nn.silu(_causal_conv(jnp.concatenate([ml_q, ml_k], axis=-1), ml_conv_w[i], ml_conv_b[i]))
        ml_q, ml_k = jnp.split(qk, 2, axis=-1)
        y_ml = _mlstm_chunkwise(_split_heads(ml_q, ML_HEADS), _split_heads(ml_k, ML_HEADS), _split_heads(ml_v, ML_HEADS), ml_i + ml_i_bias[i], ml_f + ml_f_bias[i])
        y_ml = jax.nn.sigmoid(ml_o) * _merge_heads(y_ml)
        y_ml = _head_rmsnorm(y_ml, ml_norm_w[i], ML_HEADS) * jax.nn.silu(ml_z)

        y = jnp.concatenate([y_sb, y_ml], axis=-1) @ w_out[i]
        h = h + _rmsnorm(y, post_norm_w[i])

        gate = jax.nn.sigmoid(h @ ple_w_gate[i] + ple_b_gate[i])
        h = h + gate * (p[i] @ ple_w_up[i])
    return h
```

```python
import functools

import jax
import jax.numpy as jnp
from jax import lax
from jax.experimental import pallas as pl
from jax.experimental.pallas import tpu as pltpu

SB_HEADS = 8
ML_HEADS = 4
CONV_K = 4
EPS = 1e-6

LANES = 128
SUBLANES = 8
VMEM_LIMIT_BYTES = 56 * 1024 * 1024

F32 = jnp.float32
BF16 = jnp.bfloat16


def _tiles(seq):
    def pick(pref):
        t = min(pref, seq)
        assert seq % t == 0 and t % LANES == 0
        return t
    return dict(inproj=pick(512), sb_q=pick(256), ml=pick(256), out=pick(512))


def _dot(a, b):
    return jnp.dot(a, b, preferred_element_type=F32)


def _dot_nt(a, b):
    return lax.dot_general(a, b, (((1,), (1,)), ((), ())), preferred_element_type=F32)


def _split_bf16(x):
    hi = x.astype(BF16)
    lo = (x - hi.astype(F32)).astype(BF16)
    return hi, lo


def _sigmoid(x):
    return 1.0 / (1.0 + jnp.exp(-x))


def _silu(x):
    return x * _sigmoid(x)


def _softplus(x):
    return jnp.maximum(x, 0.0) + jnp.log(1.0 + jnp.exp(-jnp.abs(x)))


def _log_sigmoid(x):
    return -_softplus(-x)


def _const_spec(shape):
    return pl.BlockSpec(shape, lambda *_: (0,) * len(shape), pipeline_mode=pl.Buffered(1))


def _inproj_kernel(x_ref, pnw_ref, w_ref, wvt_ref, wg_ref, cw_ref, cb_ref,
                   q_ref, k_ref, v_ref, gz_ref, mq_ref, mk_ref, mv_ref, mvt_ref,
                   go_ref, gmz_ref, gates_ref, conv_buf, *, ts, half, sb_scale, ml_scale):
    x = x_ref[...]
    ms = jnp.mean(x * x, axis=-1, keepdims=True)
    u = (x * lax.rsqrt(ms + EPS)) * pnw_ref[...]
    ub = u.astype(BF16)

    def seg(j):
        return _dot(ub, w_ref[:, j * half:(j + 1) * half])

    q_ref[...] = (seg(0) * sb_scale).astype(BF16)
    k_ref[...] = seg(1).astype(BF16)
    v_ref[...] = seg(2).astype(BF16)
    gz_ref[...] = _silu(seg(3)).astype(BF16)

    @pl.when(pl.program_id(1) == 0)
    def _():
        conv_buf[0:SUBLANES, :] = jnp.zeros((SUBLANES, 2 * half), F32)

    conv_buf[SUBLANES:SUBLANES + ts, 0:half] = seg(4)
    conv_buf[SUBLANES:SUBLANES + ts, half:2 * half] = seg(5)
    acc = cb_ref[...]
    for j in range(CONV_K):
        acc = acc + cw_ref[j:j + 1, :] * conv_buf[pl.ds(SUBLANES - (CONV_K - 1) + j, ts), :]
    qk = _silu(acc)
    mq_ref[...] = qk[:, :half].astype(BF16)
    mk_ref[...] = (qk[:, half:] * ml_scale).astype(BF16)
    conv_buf[0:SUBLANES, :] = conv_buf[ts:ts + SUBLANES, :]

    mv_ref[...] = seg(6).astype(BF16)
    mvt_ref[...] = _dot_nt(wvt_ref[...], ub).astype(BF16)
    go_ref[...] = _sigmoid(seg(7)).astype(BF16)
    gmz_ref[...] = _silu(seg(8)).astype(BF16)
    gates_ref[...] = _dot_nt(wg_ref[...], ub)


def _inproj(x, pre_norm_w, w_main, w_vt, w_gates, conv_w, conv_b, *, ts):
    b, s, d = x.shape
    half = d // 2
    n_gates = w_gates.shape[0]
    sb_scale = float(half // SB_HEADS) ** -0.5
    ml_scale = float(half // ML_HEADS) ** -0.5
    row = pl.BlockSpec((None, ts, half), lambda bi, i: (bi, i, 0))
    act = jax.ShapeDtypeStruct((b, s, half), BF16)
    return pl.pallas_call(
        functools.partial(_inproj_kernel, ts=ts, half=half, sb_scale=sb_scale, ml_scale=ml_scale),
        grid=(b, s // ts),
        in_specs=[
            pl.BlockSpec((None, ts, d), lambda bi, i: (bi, i, 0)),
            _const_spec((1, d)),
            _const_spec(w_main.shape),
            _const_spec(w_vt.shape),
            _const_spec(w_gates.shape),
            _const_spec(conv_w.shape),
            _const_spec((1, 2 * half)),
        ],
        out_specs=[row, row, row, row, row, row, row,
                   pl.BlockSpec((None, half, ts), lambda bi, i: (bi, 0, i)),
                   row, row,
                   pl.BlockSpec((None, n_gates, ts), lambda bi, i: (bi, 0, i))],
        out_shape=[act, act, act, act, act, act, act,
                   jax.ShapeDtypeStruct((b, half, s), BF16),
                   act, act,
                   jax.ShapeDtypeStruct((b, n_gates, s), F32)],
        scratch_shapes=[pltpu.VMEM((ts + SUBLANES, 2 * half), F32)],
        compiler_params=pltpu.CompilerParams(
            dimension_semantics=("arbitrary", "arbitrary"), vmem_limit_bytes=VMEM_LIMIT_BYTES),
        name="inproj",
    )(x, pre_norm_w.reshape(1, d), w_main, w_vt, w_gates, conv_w, conv_b.reshape(1, 2 * half))


def _sb_kernel(q_ref, k_ref, v_ref, tt_ref, y_ref, acc_ref, c0_ref, c1_ref, *, tq):
    i = pl.program_id(2)
    nsub = tq // LANES
    hd = LANES // 2
    q2 = q_ref[...]
    first_head = lax.broadcasted_iota(jnp.int32, (LANES, LANES), 1) < hd
    row = lax.broadcasted_iota(jnp.int32, (tq, LANES), 0)
    col = lax.broadcasted_iota(jnp.int32, (tq, LANES), 1)
    acc_ref[...] = jnp.zeros_like(acc_ref)
    c0_ref[...] = jnp.zeros_like(c0_ref)
    c1_ref[...] = jnp.zeros_like(c1_ref)

    def block(kb, diag_r):
        start = pl.multiple_of(kb * LANES, LANES)
        kblk = k_ref[pl.ds(start, LANES), :]
        vblk = v_ref[pl.ds(start, LANES), :]
        zero = jnp.zeros_like(kblk)
        kcat = jnp.concatenate([jnp.where(first_head, kblk, zero), jnp.where(first_head, zero, kblk)], axis=0)
        vcat = jnp.concatenate([jnp.where(first_head, vblk, zero), jnp.where(first_head, zero, vblk)], axis=0)
        z = _dot_nt(q2, kcat)
        visible = None if diag_r is None else (diag_r * LANES + col) < row
        a_parts = []
        for hh, c_ref in ((0, c0_ref), (1, c1_ref)):
            zh = z[:, hh * LANES:(hh + 1) * LANES]
            sp = _softplus(zh)
            if visible is not None:
                sp = jnp.where(visible, sp, 0.0)
            hi, lo = _split_bf16(sp)
            ct = _dot(jnp.concatenate([hi, lo], axis=1), tt_ref[...])
            carry = c_ref[...]
            a = jnp.exp(zh - (ct[:, :LANES] + carry))
            if visible is not None:
                a = jnp.where(visible, a, 0.0)
            c_ref[...] = carry + ct[:, LANES:]
            a_parts.append(a.astype(BF16))
        acc_ref[...] += _dot(jnp.concatenate(a_parts, axis=1), vcat)

    for r in reversed(range(nsub)):
        block(i * nsub + r, r)

    def body(j, carry):
        base = (i - 1 - j) * nsub
        for r in reversed(range(nsub)):
            block(base + r, None)
        return carry

    lax.fori_loop(0, i, body, 0)
    y_ref[...] = acc_ref[...]


def _sb_attention(q, k, v, *, tq):
    b, s, width = q.shape
    pairs = width // LANES
    j = lax.broadcasted_iota(jnp.int32, (LANES, LANES), 0)
    t = lax.broadcasted_iota(jnp.int32, (LANES, LANES), 1)
    tri_ones = jnp.concatenate([(j >= t).astype(BF16), jnp.ones((LANES, LANES), BF16)], axis=1)
    tt = jnp.concatenate([tri_ones, tri_ones], axis=0)
    qspec = pl.BlockSpec((None, tq, LANES), lambda bi, p, i: (bi, i, p))
    kvspec = pl.BlockSpec((None, s, LANES), lambda bi, p, i: (bi, 0, p))
    return pl.pallas_call(
        functools.partial(_sb_kernel, tq=tq),
        grid=(b, pairs, s // tq),
        in_specs=[qspec, kvspec, kvspec, _const_spec(tt.shape)],
        out_specs=qspec,
        out_shape=jax.ShapeDtypeStruct((b, s, width), F32),
        scratch_shapes=[pltpu.VMEM((tq, LANES), F32)] * 3,
        compiler_params=pltpu.CompilerParams(
            dimension_semantics=("arbitrary", "arbitrary", "arbitrary"), vmem_limit_bytes=VMEM_LIMIT_BYTES),
        name="sb_attention",
    )(q, k, v, tt)


def _mlstm_kernel(q_ref, k_ref, v_ref, vt_ref, g_ref, gb_ref, cs_ref, h_ref, st_ref, m_ref,
                  *, ts, chunk, d, nh):
    @pl.when(pl.program_id(1) == 0)
    def _():
        st_ref[...] = jnp.zeros_like(st_ref)
        m_ref[...] = jnp.zeros_like(m_ref)

    r_i = lax.broadcasted_iota(jnp.int32, (chunk, chunk), 0)
    c_i = lax.broadcasted_iota(jnp.int32, (chunk, chunk), 1)
    lower = c_i <= r_i
    eye = c_i == r_i
    ones_cols = jnp.ones((2 * chunk, d), BF16)
    ones_v = jnp.ones((chunk, d), BF16)
    ones_vt = jnp.ones((d, chunk), BF16)

    def col_rep(r):
        hi, lo = _split_bf16(jnp.where(eye, r, 0.0))
        return _dot(jnp.concatenate([hi, lo], axis=1), ones_cols)

    for c in range(ts // chunk):
        sl = slice(c * chunk, (c + 1) * chunk)
        gates = g_ref[:, sl] + gb_ref[...]
        hi, lo = _split_bf16(_log_sigmoid(gates))
        bcum = _dot(jnp.concatenate([hi, lo], axis=1), cs_ref[...])
        for h in range(nh):
            hs = slice(h * d, (h + 1) * d)
            q = q_ref[sl, hs]
            k = k_ref[sl, hs]
            v = v_ref[sl, hs]
            vt = vt_ref[hs, sl]
            ig = gates[h:h + 1, :]
            b = bcum[nh + h:nh + h + 1, :]
            u = ig - b
            b_last = b[:, chunk - 1:chunk]
            m_loc = jnp.max(b_last + u, axis=-1, keepdims=True)
            b_rep = col_rep(b)
            u_rep = col_rep(u)
            m_prev = m_ref[h, 0:1, :]
            state = st_ref[h]

            w = jnp.where(lower, u, -jnp.inf)
            m_row = jnp.maximum(jnp.max(w, axis=-1, keepdims=True), m_prev[:, 0:1])
            scores = (_dot_nt(q, k) * jnp.exp(w - m_row)).astype(BF16)
            intra = _dot(scores, jnp.concatenate([v, ones_v], axis=1))
            inter = _dot_nt(q, state.astype(BF16))
            carry_scale = jnp.exp(m_prev - m_row)
            num = intra[:, :d] + carry_scale * inter[:, :d]
            den = intra[:, d:] + carry_scale * inter[:, d:]
            h_ref[sl, hs] = num / jnp.maximum(jnp.abs(den), jnp.exp(-b_rep - m_row))

            kw = (k.astype(F32) * jnp.exp(b_last + u_rep - m_loc)).astype(BF16)
            upd = _dot(jnp.concatenate([vt, ones_vt], axis=0), kw)
            m_new = jnp.maximum(b_last + m_prev, m_loc)
            st_ref[h] = jnp.exp(b_last + m_prev - m_new) * state + jnp.exp(m_loc - m_new) * upd
            m_ref[h, 0:1, :] = m_new


def _mlstm(q, k, v, vt, gates, gate_bias, *, ts, chunk=LANES):
    b, s, width = q.shape
    nh = ML_HEADS
    d = width // nh
    assert d == LANES and chunk == LANES
    j = lax.broadcasted_iota(jnp.int32, (chunk, chunk), 0)
    t = lax.broadcasted_iota(jnp.int32, (chunk, chunk), 1)
    triu = (j <= t).astype(BF16)
    cs = jnp.concatenate([triu, triu], axis=0)
    row = pl.BlockSpec((None, ts, width), lambda bi, i: (bi, i, 0))
    return pl.pallas_call(
        functools.partial(_mlstm_kernel, ts=ts, chunk=chunk, d=d, nh=nh),
        grid=(b, s // ts),
        in_specs=[row, row, row,
                  pl.BlockSpec((None, width, ts), lambda bi, i: (bi, 0, i)),
                  pl.BlockSpec((None, 2 * nh, ts), lambda bi, i: (bi, 0, i)),
                  _const_spec((2 * nh, 1)),
                  _const_spec(cs.shape)],
        out_specs=row,
        out_shape=jax.ShapeDtypeStruct((b, s, width), F32),
        scratch_shapes=[pltpu.VMEM((nh, 2 * d, d), F32), pltpu.VMEM((nh, SUBLANES, LANES), F32)],
        compiler_params=pltpu.CompilerParams(
            dimension_semantics=("arbitrary", "arbitrary"), vmem_limit_bytes=VMEM_LIMIT_BYTES),
        name="mlstm",
    )(q, k, v, vt, gates, gate_bias, cs)


def _out_kernel(ysb_ref, gz_ref, hml_ref, go_ref, gmz_ref, x_ref, p_ref,
                sbw_ref, mlw_ref, gsb_ref, gml_ref, wout_ref, pw_ref, wgate_ref, bgate_ref, wup_ref,
                o_ref, *, half):
    ysb = ysb_ref[...]
    ms_sb = _dot((ysb * ysb).astype(BF16), gsb_ref[...])
    a = (ysb * lax.rsqrt(ms_sb + EPS)) * sbw_ref[...] * gz_ref[...].astype(F32)
    yml = go_ref[...].astype(F32) * hml_ref[...]
    ms_ml = _dot((yml * yml).astype(BF16), gml_ref[...])
    bb = (yml * lax.rsqrt(ms_ml + EPS)) * mlw_ref[...] * gmz_ref[...].astype(F32)
    y = _dot(a.astype(BF16), wout_ref[0:half, :]) + _dot(bb.astype(BF16), wout_ref[half:2 * half, :])
    ms_y = jnp.mean(y * y, axis=-1, keepdims=True)
    h = x_ref[...] + (y * lax.rsqrt(ms_y + EPS)) * pw_ref[...]
    gate = _sigmoid(_dot(h.astype(BF16), wgate_ref[...]) + bgate_ref[...])
    o_ref[...] = h + gate * _dot(p_ref[...].astype(BF16), wup_ref[...])


def _head_mean_matrix(width, heads):
    hd = width // heads
    j = lax.broadcasted_iota(jnp.int32, (width, width), 0) // hd
    t = lax.broadcasted_iota(jnp.int32, (width, width), 1) // hd
    return jnp.where(j == t, 1.0 / hd, 0.0).astype(BF16)


def _out_stage(ysb, gz, hml, go, gmz, x, p, sb_norm_w, ml_norm_w, w_out, post_norm_w,
               w_gate, b_gate, w_up, *, ts):
    t, d = x.shape
    half = d // 2
    pd = p.shape[-1]
    rows = lambda w: pl.BlockSpec((ts, w), lambda i: (i, 0))
    return pl.pallas_call(
        functools.partial(_out_kernel, half=half),
        grid=(t // ts,),
        in_specs=[rows(half), rows(half), rows(half), rows(half), rows(half), rows(d), rows(pd),
                  _const_spec((1, half)), _const_spec((1, half)),
                  _const_spec((half, half)), _const_spec((half, half)),
                  _const_spec((d, d)), _const_spec((1, d)),
                  _const_spec((d, d)), _const_spec((1, d)), _const_spec((pd, d))],
        out_specs=rows(d),
        out_shape=jax.ShapeDtypeStruct((t, d), F32),
        compiler_params=pltpu.CompilerParams(
            dimension_semantics=("arbitrary",), vmem_limit_bytes=VMEM_LIMIT_BYTES),
        name="out_stage",
    )(ysb, gz, hml, go, gmz, x, p,
      sb_norm_w.reshape(1, half), ml_norm_w.reshape(1, half),
      _head_mean_matrix(half, SB_HEADS), _head_mean_matrix(half, ML_HEADS),
      w_out, post_norm_w.reshape(1, d), w_gate, b_gate.reshape(1, d), w_up)


def _layer(h, p, pre_norm_w, w_in, ml_conv_w, ml_conv_b, ml_i_bias, ml_f_bias, sb_norm_w, ml_norm_w,
           w_out, post_norm_w, ple_w_up, ple_w_gate, ple_b_gate):
    b, s, d = h.shape
    half = d // 2
    tiles = _tiles(s)
    wb = w_in.astype(BF16)
    n_main = 9 * half
    q, k, v, gz, mq, mk, mv, mvt, go, gmz, gates = _inproj(
        h, pre_norm_w, wb[:, :n_main], wb[:, 6 * half:7 * half].T, wb[:, n_main:].T,
        ml_conv_w, ml_conv_b, ts=tiles["inproj"])
    ysb = _sb_attention(q, k, v, tq=tiles["sb_q"])
    gate_bias = jnp.concatenate([ml_i_bias, ml_f_bias]).reshape(2 * ML_HEADS, 1)
    hml = _mlstm(mq, mk, mv, mvt, gates, gate_bias, ts=tiles["ml"])
    flat = lambda a: a.reshape(b * s, a.shape[-1])
    out = _out_stage(flat(ysb), flat(gz), flat(hml), flat(go), flat(gmz), flat(h), flat(p),
                     sb_norm_w, ml_norm_w, w_out.astype(BF16), post_norm_w,
                     ple_w_gate.astype(BF16), ple_b_gate, ple_w_up.astype(BF16), ts=tiles["out"])
    return out.reshape(b, s, d)


def kernel(x, p, pre_norm_w, w_in, ml_conv_w, ml_conv_b, ml_i_bias, ml_f_bias, sb_norm_w, ml_norm_w,
           w_out, post_norm_w, ple_w_up, ple_w_gate, ple_b_gate):
    h = x
    for i in range(pre_norm_w.shape[0]):
        h = _layer(h, p[i], pre_norm_w[i], w_in[i], ml_conv_w[i], ml_conv_b[i], ml_i_bias[i], ml_f_bias[i],
                   sb_norm_w[i], ml_norm_w[i], w_out[i], post_norm_w[i], ple_w_up[i], ple_w_gate[i],
                   ple_b_gate[i])
    return h
```

```python
import functools

import jax
import jax.numpy as jnp
from jax import lax
from jax.experimental import pallas as pl
from jax.experimental.pallas import tpu as pltpu

SB_HEADS = 8
ML_HEADS = 4
CONV_K = 4
EPS = 1e-6
LOG2_E = 1.4426950408889634

LANES = 128
SUBLANES = 8
VMEM_LIMIT_BYTES = 56 * 1024 * 1024

F32 = jnp.float32
BF16 = jnp.bfloat16


def _tiles(seq):
    def pick(pref):
        t = min(pref, seq)
        assert seq % t == 0 and t % LANES == 0
        return t
    return dict(inproj=pick(512), sb_q=pick(512), ml=pick(256), out=pick(512))


def _dot(a, b):
    return jnp.dot(a, b, preferred_element_type=F32)


def _dot_nt(a, b):
    return lax.dot_general(a, b, (((1,), (1,)), ((), ())), preferred_element_type=F32)


def _split_bf16(x):
    hi = x.astype(BF16)
    lo = (x - hi.astype(F32)).astype(BF16)
    return hi, lo


def _sigmoid(x):
    return 1.0 / (1.0 + jnp.exp(-x))


def _silu(x):
    return x * _sigmoid(x)


def _softplus(x):
    return jnp.maximum(x, 0.0) + jnp.log(1.0 + jnp.exp(-jnp.abs(x)))


def _log_sigmoid(x):
    return -_softplus(-x)


def _softplus2(x):
    return jnp.maximum(x, 0.0) + jnp.log2(1.0 + jnp.exp2(-jnp.abs(x)))


def _const_spec(shape):
    return pl.BlockSpec(shape, lambda *_: (0,) * len(shape), pipeline_mode=pl.Buffered(1))


def _store_pair_masked(dst_ref, a, ts):
    first_head = lax.broadcasted_iota(jnp.int32, (LANES, LANES), 1) < LANES // 2
    zero = jnp.zeros((LANES, LANES), a.dtype)
    for p in range(a.shape[1] // LANES):
        for j in range(ts // LANES):
            blk = a[j * LANES:(j + 1) * LANES, p * LANES:(p + 1) * LANES]
            dst_ref[p, j, 0:LANES, :] = jnp.where(first_head, blk, zero)
            dst_ref[p, j, LANES:2 * LANES, :] = jnp.where(first_head, zero, blk)


def _inproj_kernel(x_ref, pnw_ref, w_ref, wvt_ref, wg_ref, cw_ref, cb_ref,
                   q_ref, kc_ref, vc_ref, gz_ref, mq_ref, mk_ref, mv_ref, mvt_ref,
                   go_ref, gmz_ref, gates_ref, conv_buf, *, ts, half, sb_scale, ml_scale):
    x = x_ref[...]
    ms = jnp.mean(x * x, axis=-1, keepdims=True)
    u = (x * lax.rsqrt(ms + EPS)) * pnw_ref[...]
    ub = u.astype(BF16)

    def seg(j):
        return _dot(ub, w_ref[:, j * half:(j + 1) * half])

    q_ref[...] = (seg(0) * sb_scale).astype(BF16)
    _store_pair_masked(kc_ref, seg(1).astype(BF16), ts)
    _store_pair_masked(vc_ref, seg(2).astype(BF16), ts)
    gz_ref[...] = _silu(seg(3)).astype(BF16)

    @pl.when(pl.program_id(1) == 0)
    def _():
        conv_buf[0:SUBLANES, :] = jnp.zeros((SUBLANES, 2 * half), F32)

    conv_buf[SUBLANES:SUBLANES + ts, 0:half] = seg(4)
    conv_buf[SUBLANES:SUBLANES + ts, half:2 * half] = seg(5)
    acc = cb_ref[...]
    for j in range(CONV_K):
        acc = acc + cw_ref[j:j + 1, :] * conv_buf[pl.ds(SUBLANES - (CONV_K - 1) + j, ts), :]
    qk = _silu(acc)
    mq_ref[...] = qk[:, :half].astype(BF16)
    mk_ref[...] = (qk[:, half:] * ml_scale).astype(BF16)
    conv_buf[0:SUBLANES, :] = conv_buf[ts:ts + SUBLANES, :]

    mv_ref[...] = seg(6).astype(BF16)
    mvt_ref[...] = _dot_nt(wvt_ref[...], ub).astype(BF16)
    go_ref[...] = _sigmoid(seg(7)).astype(BF16)
    gmz_ref[...] = _silu(seg(8)).astype(BF16)
    gates_ref[...] = _dot_nt(wg_ref[...], ub)


def _inproj(x, pre_norm_w, w_main, w_vt, w_gates, conv_w, conv_b, *, ts):
    b, s, d = x.shape
    half = d // 2
    n_gates = w_gates.shape[0]
    sb_scale = float(half // SB_HEADS) ** -0.5 * LOG2_E
    ml_scale = float(half // ML_HEADS) ** -0.5
    row = pl.BlockSpec((None, ts, half), lambda bi, i: (bi, i, 0))
    act = jax.ShapeDtypeStruct((b, s, half), BF16)
    pairs = half // LANES
    pair_blocks = pl.BlockSpec((None, pairs, ts // LANES, 2 * LANES, LANES), lambda bi, i: (bi, 0, i, 0, 0))
    pair_act = jax.ShapeDtypeStruct((b, pairs, s // LANES, 2 * LANES, LANES), BF16)
    return pl.pallas_call(
        functools.partial(_inproj_kernel, ts=ts, half=half, sb_scale=sb_scale, ml_scale=ml_scale),
        grid=(b, s // ts),
        in_specs=[
            pl.BlockSpec((None, ts, d), lambda bi, i: (bi, i, 0)),
            _const_spec((1, d)),
            _const_spec(w_main.shape),
            _const_spec(w_vt.shape),
            _const_spec(w_gates.shape),
            _const_spec(conv_w.shape),
            _const_spec((1, 2 * half)),
        ],
        out_specs=[row, pair_blocks, pair_blocks, row, row, row, row,
                   pl.BlockSpec((None, half, ts), lambda bi, i: (bi, 0, i)),
                   row, row,
                   pl.BlockSpec((None, n_gates, ts), lambda bi, i: (bi, 0, i))],
        out_shape=[act, pair_act, pair_act, act, act, act, act,
                   jax.ShapeDtypeStruct((b, half, s), BF16),
                   act, act,
                   jax.ShapeDtypeStruct((b, n_gates, s), F32)],
        scratch_shapes=[pltpu.VMEM((ts + SUBLANES, 2 * half), F32)],
        compiler_params=pltpu.CompilerParams(
            dimension_semantics=("arbitrary", "arbitrary"), vmem_limit_bytes=VMEM_LIMIT_BYTES),
        name="inproj",
    )(x, pre_norm_w.reshape(1, d), w_main, w_vt, w_gates, conv_w, conv_b.reshape(1, 2 * half))


def _sb_kernel(q_ref, kc_ref, vc_ref, tt_ref, y_ref, acc_ref, c0_ref, c1_ref, *, tq):
    i = pl.program_id(2)
    nsub = tq // LANES
    acc_ref[...] = jnp.zeros_like(acc_ref)
    c0_ref[...] = jnp.zeros_like(c0_ref)
    c1_ref[...] = jnp.zeros_like(c1_ref)

    def block(kb, r0, masked):
        rows = slice(r0, tq)
        z = _dot_nt(q_ref[rows, :], kc_ref[kb])
        if masked:
            n = tq - r0
            visible = (lax.broadcasted_iota(jnp.int32, (n, LANES), 1)
                       < lax.broadcasted_iota(jnp.int32, (n, LANES), 0))
        a_parts = []
        for hh, c_ref in ((0, c0_ref), (1, c1_ref)):
            zh = z[:, hh * LANES:(hh + 1) * LANES]
            sp = _softplus2(zh)
            if masked:
                sp = jnp.where(visible, sp, 0.0)
            hi, lo = _split_bf16(sp)
            ct = _dot(jnp.concatenate([hi, lo], axis=1), tt_ref[...])
            carry = c_ref[rows, :]
            a = jnp.exp2(zh - (ct[:, :LANES] + carry))
            if masked:
                a = jnp.where(visible, a, 0.0)
            c_ref[rows, :] = carry + ct[:, LANES:]
            a_parts.append(a.astype(BF16))
        acc_ref[rows, :] += _dot(jnp.concatenate(a_parts, axis=1), vc_ref[kb])

    for r in reversed(range(nsub)):
        block(i * nsub + r, r * LANES, True)

    def body(j, carry):
        base = (i - 1 - j) * nsub
        for r in reversed(range(nsub)):
            block(base + r, 0, False)
        return carry

    lax.fori_loop(0, i, body, 0)
    y_ref[...] = acc_ref[...]


def _sb_attention(q, kc, vc, *, tq):
    b, s, width = q.shape
    pairs = width // LANES
    j = lax.broadcasted_iota(jnp.int32, (LANES, LANES), 0)
    t = lax.broadcasted_iota(jnp.int32, (LANES, LANES), 1)
    tri_ones = jnp.concatenate([(j >= t).astype(BF16), jnp.ones((LANES, LANES), BF16)], axis=1)
    tt = jnp.concatenate([tri_ones, tri_ones], axis=0)
    qspec = pl.BlockSpec((None, tq, LANES), lambda bi, p, i: (bi, i, p))
    kvspec = pl.BlockSpec((None, None, s // LANES, 2 * LANES, LANES), lambda bi, p, i: (bi, p, 0, 0, 0))
    return pl.pallas_call(
        functools.partial(_sb_kernel, tq=tq),
        grid=(b, pairs, s // tq),
        in_specs=[qspec, kvspec, kvspec, _const_spec(tt.shape)],
        out_specs=qspec,
        out_shape=jax.ShapeDtypeStruct((b, s, width), F32),
        scratch_shapes=[pltpu.VMEM((tq, LANES), F32)] * 3,
        compiler_params=pltpu.CompilerParams(
            dimension_semantics=("arbitrary", "arbitrary", "arbitrary"), vmem_limit_bytes=VMEM_LIMIT_BYTES),
        name="sb_attention",
    )(q, kc, vc, tt)


def _mlstm_kernel(q_ref, k_ref, v_ref, vt_ref, g_ref, gb_ref, cs_ref, h_ref, st_ref, m_ref,
                  *, ts, chunk, d, nh):
    @pl.when(pl.program_id(1) == 0)
    def _():
        st_ref[...] = jnp.zeros_like(st_ref)
        m_ref[...] = jnp.zeros_like(m_ref)

    r_i = lax.broadcasted_iota(jnp.int32, (chunk, chunk), 0)
    c_i = lax.broadcasted_iota(jnp.int32, (chunk, chunk), 1)
    lower = c_i <= r_i
    eye = c_i == r_i
    ones_cols = jnp.ones((2 * chunk, d), BF16)
    ones_v = jnp.ones((chunk, d), BF16)
    ones_vt = jnp.ones((d, chunk), BF16)

    def col_rep(r):
        hi, lo = _split_bf16(jnp.where(eye, r, 0.0))
        return _dot(jnp.concatenate([hi, lo], axis=1), ones_cols)

    for c in range(ts // chunk):
        sl = slice(c * chunk, (c + 1) * chunk)
        gates = g_ref[:, sl] + gb_ref[...]
        hi, lo = _split_bf16(_log_sigmoid(gates))
        bcum = _dot(jnp.concatenate([hi, lo], axis=1), cs_ref[...])
        for h in range(nh):
            hs = slice(h * d, (h + 1) * d)
            q = q_ref[sl, hs]
            k = k_ref[sl, hs]
            v = v_ref[sl, hs]
            vt = vt_ref[hs, sl]
            ig = gates[h:h + 1, :]
            b = bcum[nh + h:nh + h + 1, :]
            u = ig - b
            b_last = b[:, chunk - 1:chunk]
            m_loc = jnp.max(b_last + u, axis=-1, keepdims=True)
            b_rep = col_rep(b)
            u_rep = col_rep(u)
            m_prev = m_ref[h, 0:1, :]
            state = st_ref[h]

            w = jnp.where(lower, u, -jnp.inf)
            m_row = jnp.maximum(jnp.max(w, axis=-1, keepdims=True), m_prev[:, 0:1])
            scores = (_dot_nt(q, k) * jnp.exp(w - m_row)).astype(BF16)
            intra = _dot(scores, jnp.concatenate([v, ones_v], axis=1))
            inter = _dot_nt(q, state.astype(BF16))
            carry_scale = jnp.exp(m_prev - m_row)
            num = intra[:, :d] + carry_scale * inter[:, :d]
            den = intra[:, d:] + carry_scale * inter[:, d:]
            h_ref[sl, hs] = num / jnp.maximum(jnp.abs(den), jnp.exp(-b_rep - m_row))

            kw = (k.astype(F32) * jnp.exp(b_last + u_rep - m_loc)).astype(BF16)
            upd = _dot(jnp.concatenate([vt, ones_vt], axis=0), kw)
            m_new = jnp.maximum(b_last + m_prev, m_loc)
            st_ref[h] = jnp.exp(b_last + m_prev - m_new) * state + jnp.exp(m_loc - m_new) * upd
            m_ref[h, 0:1, :] = m_new


def _mlstm(q, k, v, vt, gates, gate_bias, *, ts, chunk=LANES):
    b, s, width = q.shape
    nh = ML_HEADS
    d = width // nh
    assert d == LANES and chunk == LANES
    j = lax.broadcasted_iota(jnp.int32, (chunk, chunk), 0)
    t = lax.broadcasted_iota(jnp.int32, (chunk, chunk), 1)
    triu = (j <= t).astype(BF16)
    cs = jnp.concatenate([triu, triu], axis=0)
    row = pl.BlockSpec((None, ts, width), lambda bi, i: (bi, i, 0))
    return pl.pallas_call(
        functools.partial(_mlstm_kernel, ts=ts, chunk=chunk, d=d, nh=nh),
        grid=(b, s // ts),
        in_specs=[row, row, row,
                  pl.BlockSpec((None, width, ts), lambda bi, i: (bi, 0, i)),
                  pl.BlockSpec((None, 2 * nh, ts), lambda bi, i: (bi, 0, i)),
                  _const_spec((2 * nh, 1)),
                  _const_spec(cs.shape)],
        out_specs=row,
        out_shape=jax.ShapeDtypeStruct((b, s, width), F32),
        scratch_shapes=[pltpu.VMEM((nh, 2 * d, d), F32), pltpu.VMEM((nh, SUBLANES, LANES), F32)],
        compiler_params=pltpu.CompilerParams(
            dimension_semantics=("arbitrary", "arbitrary"), vmem_limit_bytes=VMEM_LIMIT_BYTES),
        name="mlstm",
    )(q, k, v, vt, gates, gate_bias, cs)


def _out_kernel(ysb_ref, gz_ref, hml_ref, go_ref, gmz_ref, x_ref, p_ref,
                sbw_ref, mlw_ref, gsb_ref, gml_ref, wout_ref, pw_ref, wgate_ref, bgate_ref, wup_ref,
                o_ref, *, half):
    ysb = ysb_ref[...]
    ms_sb = _dot((ysb * ysb).astype(BF16), gsb_ref[...])
    a = (ysb * lax.rsqrt(ms_sb + EPS)) * sbw_ref[...] * gz_ref[...].astype(F32)
    yml = go_ref[...].astype(F32) * hml_ref[...]
    ms_ml = _dot((yml * yml).astype(BF16), gml_ref[...])
    bb = (yml * lax.rsqrt(ms_ml + EPS)) * mlw_ref[...] * gmz_ref[...].astype(F32)
    y = _dot(a.astype(BF16), wout_ref[0:half, :]) + _dot(bb.astype(BF16), wout_ref[half:2 * half, :])
    ms_y = jnp.mean(y * y, axis=-1, keepdims=True)
    h = x_ref[...] + (y * lax.rsqrt(ms_y + EPS)) * pw_ref[...]
    gate = _sigmoid(_dot(h.astype(BF16), wgate_ref[...]) + bgate_ref[...])
    o_ref[...] = h + gate * _dot(p_ref[...].astype(BF16), wup_ref[...])


def _head_mean_matrix(width, heads):
    hd = width // heads
    j = lax.broadcasted_iota(jnp.int32, (width, width), 0) // hd
    t = lax.broadcasted_iota(jnp.int32, (width, width), 1) // hd
    return jnp.where(j == t, 1.0 / hd, 0.0).astype(BF16)


def _out_stage(ysb, gz, hml, go, gmz, x, p, sb_norm_w, ml_norm_w, w_out, post_norm_w,
               w_gate, b_gate, w_up, *, ts):
    t, d = x.shape
    half = d // 2
    pd = p.shape[-1]
    rows = lambda w: pl.BlockSpec((ts, w), lambda i: (i, 0))
    return pl.pallas_call(
        functools.partial(_out_kernel, half=half),
        grid=(t // ts,),
        in_specs=[rows(half), rows(half), rows(half), rows(half), rows(half), rows(d), rows(pd),
                  _const_spec((1, half)), _const_spec((1, half)),
                  _const_spec((half, half)), _const_spec((half, half)),
                  _const_spec((d, d)), _const_spec((1, d)),
                  _const_spec((d, d)), _const_spec((1, d)), _const_spec((pd, d))],
        out_specs=rows(d),
        out_shape=jax.ShapeDtypeStruct((t, d), F32),
        compiler_params=pltpu.CompilerParams(
            dimension_semantics=("arbitrary",), vmem_limit_bytes=VMEM_LIMIT_BYTES),
        name="out_stage",
    )(ysb, gz, hml, go, gmz, x, p,
      sb_norm_w.reshape(1, half), ml_norm_w.reshape(1, half),
      _head_mean_matrix(half, SB_HEADS), _head_mean_matrix(half, ML_HEADS),
      w_out, post_norm_w.reshape(1, d), w_gate, b_gate.reshape(1, d), w_up)


def _layer(h, p, pre_norm_w, w_in, ml_conv_w, ml_conv_b, ml_i_bias, ml_f_bias, sb_norm_w, ml_norm_w,
           w_out, post_norm_w, ple_w_up, ple_w_gate, ple_b_gate):
    b, s, d = h.shape
    half = d // 2
    tiles = _tiles(s)
    wb = w_in.astype(BF16)
    n_main = 9 * half
    q, kc, vc, gz, mq, mk, mv, mvt, go, gmz, gates = _inproj(
        h, pre_norm_w, wb[:, :n_main], wb[:, 6 * half:7 * half].T, wb[:, n_main:].T,
        ml_conv_w, ml_conv_b, ts=tiles["inproj"])
    ysb = _sb_attention(q, kc, vc, tq=tiles["sb_q"])
    gate_bias = jnp.concatenate([ml_i_bias, ml_f_bias]).reshape(2 * ML_HEADS, 1)
    hml = _mlstm(mq, mk, mv, mvt, gates, gate_bias, ts=tiles["ml"])
    flat = lambda a: a.reshape(b * s, a.shape[-1])
    out = _out_stage(flat(ysb), flat(gz), flat(hml), flat(go), flat(gmz), flat(h), flat(p),
                     sb_norm_w, ml_norm_w, w_out.astype(BF16), post_norm_w,
                     ple_w_gate.astype(BF16), ple_b_gate, ple_w_up.astype(BF16), ts=tiles["out"])
    return out.reshape(b, s, d)


def kernel(x, p, pre_norm_w, w_in, ml_conv_w, ml_conv_b, ml_i_bias, ml_f_bias, sb_norm_w, ml_norm_w,
           w_out, post_norm_w, ple_w_up, ple_w_gate, ple_b_gate):
    h = x
    for i in range(pre_norm_w.shape[0]):
        h = _layer(h, p[i], pre_norm_w[i], w_in[i], ml_conv_w[i], ml_conv_b[i], ml_i_bias[i], ml_f_bias[i],
                   sb_norm_w[i], ml_norm_w[i], w_out[i], post_norm_w[i], ple_w_up[i], ple_w_gate[i],
                   ple_b_gate[i])
    return h
```

```python
import functools

import jax
import jax.numpy as jnp
from jax import lax
from jax.experimental import pallas as pl
from jax.experimental.pallas import tpu as pltpu

SB_HEADS = 8
ML_HEADS = 4
CONV_K = 4
EPS = 1e-6
LOG2_E = 1.4426950408889634
SB_DEAD_CARRY = 152.0

LANES = 128
SUBLANES = 8
VMEM_LIMIT_BYTES = 56 * 1024 * 1024

F32 = jnp.float32
BF16 = jnp.bfloat16


def _tiles(seq):
    def pick(pref):
        t = min(pref, seq)
        assert seq % t == 0 and t % LANES == 0
        return t
    return dict(inproj=pick(512), sb_q=pick(512), ml=pick(256), out=pick(512))


def _dot(a, b):
    return jnp.dot(a, b, preferred_element_type=F32)


def _dot_nt(a, b):
    return lax.dot_general(a, b, (((1,), (1,)), ((), ())), preferred_element_type=F32)


def _split_bf16(x):
    hi = x.astype(BF16)
    lo = (x - hi.astype(F32)).astype(BF16)
    return hi, lo


def _sigmoid(x):
    return 1.0 / (1.0 + jnp.exp(-x))


def _silu(x):
    return x * _sigmoid(x)


def _softplus(x):
    return jnp.maximum(x, 0.0) + jnp.log(1.0 + jnp.exp(-jnp.abs(x)))


def _log_sigmoid(x):
    return -_softplus(-x)


def _softplus2(x):
    return jnp.maximum(x, 0.0) + jnp.log2(1.0 + jnp.exp2(-jnp.abs(x)))


def _const_spec(shape):
    return pl.BlockSpec(shape, lambda *_: (0,) * len(shape), pipeline_mode=pl.Buffered(1))


def _store_pair_masked(dst_ref, a, ts):
    first_head = lax.broadcasted_iota(jnp.int32, (LANES, LANES), 1) < LANES // 2
    zero = jnp.zeros((LANES, LANES), a.dtype)
    for p in range(a.shape[1] // LANES):
        for j in range(ts // LANES):
            blk = a[j * LANES:(j + 1) * LANES, p * LANES:(p + 1) * LANES]
            dst_ref[p, j, 0:LANES, :] = jnp.where(first_head, blk, zero)
            dst_ref[p, j, LANES:2 * LANES, :] = jnp.where(first_head, zero, blk)


def _inproj_kernel(x_ref, pnw_ref, w_ref, wvt_ref, wg_ref, cw_ref, cb_ref,
                   q_ref, kc_ref, vc_ref, gz_ref, mq_ref, mk_ref, mv_ref, mvt_ref,
                   go_ref, gmz_ref, gates_ref, conv_buf, *, ts, half, sb_scale, ml_scale):
    x = x_ref[...]
    ms = jnp.mean(x * x, axis=-1, keepdims=True)
    u = (x * lax.rsqrt(ms + EPS)) * pnw_ref[...]
    ub = u.astype(BF16)

    def seg(j):
        return _dot(ub, w_ref[:, j * half:(j + 1) * half])

    q_ref[...] = (seg(0) * sb_scale).astype(BF16)
    _store_pair_masked(kc_ref, seg(1).astype(BF16), ts)
    _store_pair_masked(vc_ref, seg(2).astype(BF16), ts)
    gz_ref[...] = _silu(seg(3)).astype(BF16)

    @pl.when(pl.program_id(1) == 0)
    def _():
        conv_buf[0:SUBLANES, :] = jnp.zeros((SUBLANES, 2 * half), F32)

    conv_buf[SUBLANES:SUBLANES + ts, 0:half] = seg(4)
    conv_buf[SUBLANES:SUBLANES + ts, half:2 * half] = seg(5)
    acc = cb_ref[...]
    for j in range(CONV_K):
        acc = acc + cw_ref[j:j + 1, :] * conv_buf[pl.ds(SUBLANES - (CONV_K - 1) + j, ts), :]
    qk = _silu(acc)
    mq_ref[...] = qk[:, :half].astype(BF16)
    mk_ref[...] = (qk[:, half:] * ml_scale).astype(BF16)
    conv_buf[0:SUBLANES, :] = conv_buf[ts:ts + SUBLANES, :]

    mv_ref[...] = seg(6).astype(BF16)
    mvt_ref[...] = _dot_nt(wvt_ref[...], ub).astype(BF16)
    go_ref[...] = _sigmoid(seg(7)).astype(BF16)
    gmz_ref[...] = _silu(seg(8)).astype(BF16)
    gates_ref[...] = _dot_nt(wg_ref[...], ub)


def _inproj(x, pre_norm_w, w_main, w_vt, w_gates, conv_w, conv_b, *, ts):
    b, s, d = x.shape
    half = d // 2
    n_gates = w_gates.shape[0]
    sb_scale = float(half // SB_HEADS) ** -0.5 * LOG2_E
    ml_scale = float(half // ML_HEADS) ** -0.5
    row = pl.BlockSpec((None, ts, half), lambda bi, i: (bi, i, 0))
    act = jax.ShapeDtypeStruct((b, s, half), BF16)
    pairs = half // LANES
    pair_blocks = pl.BlockSpec((None, pairs, ts // LANES, 2 * LANES, LANES), lambda bi, i: (bi, 0, i, 0, 0))
    pair_act = jax.ShapeDtypeStruct((b, pairs, s // LANES, 2 * LANES, LANES), BF16)
    return pl.pallas_call(
        functools.partial(_inproj_kernel, ts=ts, half=half, sb_scale=sb_scale, ml_scale=ml_scale),
        grid=(b, s // ts),
        in_specs=[
            pl.BlockSpec((None, ts, d), lambda bi, i: (bi, i, 0)),
            _const_spec((1, d)),
            _const_spec(w_main.shape),
            _const_spec(w_vt.shape),
            _const_spec(w_gates.shape),
            _const_spec(conv_w.shape),
            _const_spec((1, 2 * half)),
        ],
        out_specs=[row, pair_blocks, pair_blocks, row, row, row, row,
                   pl.BlockSpec((None, half, ts), lambda bi, i: (bi, 0, i)),
                   row, row,
                   pl.BlockSpec((None, n_gates, ts), lambda bi, i: (bi, 0, i))],
        out_shape=[act, pair_act, pair_act, act, act, act, act,
                   jax.ShapeDtypeStruct((b, half, s), BF16),
                   act, act,
                   jax.ShapeDtypeStruct((b, n_gates, s), F32)],
        scratch_shapes=[pltpu.VMEM((ts + SUBLANES, 2 * half), F32)],
        compiler_params=pltpu.CompilerParams(
            dimension_semantics=("arbitrary", "arbitrary"), vmem_limit_bytes=VMEM_LIMIT_BYTES),
        name="inproj",
    )(x, pre_norm_w.reshape(1, d), w_main, w_vt, w_gates, conv_w, conv_b.reshape(1, 2 * half))


def _sb_kernel(q_ref, kc_ref, vc_ref, tt_ref, y_ref, acc_ref, c0_ref, c1_ref, *, tq):
    i = pl.program_id(2)
    nsub = tq // LANES
    acc_ref[...] = jnp.zeros_like(acc_ref)
    c0_ref[...] = jnp.zeros_like(c0_ref)
    c1_ref[...] = jnp.zeros_like(c1_ref)

    def block(kb, r0, masked):
        rows = slice(r0, tq)
        z = _dot_nt(q_ref[rows, :], kc_ref[kb])
        if masked:
            n = tq - r0
            visible = (lax.broadcasted_iota(jnp.int32, (n, LANES), 1)
                       < lax.broadcasted_iota(jnp.int32, (n, LANES), 0))
        a_parts = []
        for hh, c_ref in ((0, c0_ref), (1, c1_ref)):
            zh = z[:, hh * LANES:(hh + 1) * LANES]
            sp = _softplus2(zh)
            if masked:
                sp = jnp.where(visible, sp, 0.0)
            hi, lo = _split_bf16(sp)
            ct = _dot(jnp.concatenate([hi, lo], axis=1), tt_ref[...])
            carry = c_ref[rows, :]
            a = jnp.exp2(zh - (ct[:, :LANES] + carry))
            if masked:
                a = jnp.where(visible, a, 0.0)
            c_ref[rows, :] = carry + ct[:, LANES:]
            a_parts.append(a.astype(BF16))
        acc_ref[rows, :] += _dot(jnp.concatenate(a_parts, axis=1), vc_ref[kb])

    for r in reversed(range(nsub)):
        block(i * nsub + r, r * LANES, True)

    def body(state):
        kb, _ = state
        block(kb, 0, False)
        return kb - 1, jnp.minimum(jnp.min(c0_ref[...]), jnp.min(c1_ref[...]))

    def live(state):
        kb, min_carry = state
        return jnp.logical_and(kb >= 0, min_carry < SB_DEAD_CARRY)

    lax.while_loop(live, body, (i * nsub - 1, jnp.float32(0.0)))
    y_ref[...] = acc_ref[...]


def _sb_attention(q, kc, vc, *, tq):
    b, s, width = q.shape
    pairs = width // LANES
    j = lax.broadcasted_iota(jnp.int32, (LANES, LANES), 0)
    t = lax.broadcasted_iota(jnp.int32, (LANES, LANES), 1)
    tri_ones = jnp.concatenate([(j >= t).astype(BF16), jnp.ones((LANES, LANES), BF16)], axis=1)
    tt = jnp.concatenate([tri_ones, tri_ones], axis=0)
    qspec = pl.BlockSpec((None, tq, LANES), lambda bi, p, i: (bi, i, p))
    kvspec = pl.BlockSpec((None, None, s // LANES, 2 * LANES, LANES), lambda bi, p, i: (bi, p, 0, 0, 0))
    return pl.pallas_call(
        functools.partial(_sb_kernel, tq=tq),
        grid=(b, pairs, s // tq),
        in_specs=[qspec, kvspec, kvspec, _const_spec(tt.shape)],
        out_specs=qspec,
        out_shape=jax.ShapeDtypeStruct((b, s, width), F32),
        scratch_shapes=[pltpu.VMEM((tq, LANES), F32)] * 3,
        compiler_params=pltpu.CompilerParams(
            dimension_semantics=("arbitrary", "arbitrary", "arbitrary"), vmem_limit_bytes=VMEM_LIMIT_BYTES),
        name="sb_attention",
    )(q, kc, vc, tt)


def _mlstm_kernel(q_ref, k_ref, v_ref, vt_ref, g_ref, gb_ref, cs_ref, h_ref, st_ref, m_ref,
                  *, ts, chunk, d, nh):
    @pl.when(pl.program_id(1) == 0)
    def _():
        st_ref[...] = jnp.zeros_like(st_ref)
        m_ref[...] = jnp.zeros_like(m_ref)

    r_i = lax.broadcasted_iota(jnp.int32, (chunk, chunk), 0)
    c_i = lax.broadcasted_iota(jnp.int32, (chunk, chunk), 1)
    lower = c_i <= r_i
    eye = c_i == r_i
    ones_cols = jnp.ones((2 * chunk, d), BF16)
    ones_v = jnp.ones((chunk, d), BF16)
    ones_vt = jnp.ones((d, chunk), BF16)

    def col_rep(r):
        hi, lo = _split_bf16(jnp.where(eye, r, 0.0))
        return _dot(jnp.concatenate([hi, lo], axis=1), ones_cols)

    for c in range(ts // chunk):
        sl = slice(c * chunk, (c + 1) * chunk)
        gates = g_ref[:, sl] + gb_ref[...]
        hi, lo = _split_bf16(_log_sigmoid(gates))
        bcum = _dot(jnp.concatenate([hi, lo], axis=1), cs_ref[...])
        for h in range(nh):
            hs = slice(h * d, (h + 1) * d)
            q = q_ref[sl, hs]
            k = k_ref[sl, hs]
            v = v_ref[sl, hs]
            vt = vt_ref[hs, sl]
            ig = gates[h:h + 1, :]
            b = bcum[nh + h:nh + h + 1, :]
            u = ig - b
            b_last = b[:, chunk - 1:chunk]
            m_loc = jnp.max(b_last + u, axis=-1, keepdims=True)
            b_rep = col_rep(b)
            u_rep = col_rep(u)
            m_prev = m_ref[h, 0:1, :]
            state = st_ref[h]

            w = jnp.where(lower, u, -jnp.inf)
            m_row = jnp.maximum(jnp.max(w, axis=-1, keepdims=True), m_prev[:, 0:1])
            scores = (_dot_nt(q, k) * jnp.exp(w - m_row)).astype(BF16)
            intra = _dot(scores, jnp.concatenate([v, ones_v], axis=1))
            inter = _dot_nt(q, state.astype(BF16))
            carry_scale = jnp.exp(m_prev - m_row)
            num = intra[:, :d] + carry_scale * inter[:, :d]
            den = intra[:, d:] + carry_scale * inter[:, d:]
            h_ref[sl, hs] = num / jnp.maximum(jnp.abs(den), jnp.exp(-b_rep - m_row))

            kw = (k.astype(F32) * jnp.exp(b_last + u_rep - m_loc)).astype(BF16)
            upd = _dot(jnp.concatenate([vt, ones_vt], axis=0), kw)
            m_new = jnp.maximum(b_last + m_prev, m_loc)
            st_ref[h] = jnp.exp(b_last + m_prev - m_new) * state + jnp.exp(m_loc - m_new) * upd
            m_ref[h, 0:1, :] = m_new


def _mlstm(q, k, v, vt, gates, gate_bias, *, ts, chunk=LANES):
    b, s, width = q.shape
    nh = ML_HEADS
    d = width // nh
    assert d == LANES and chunk == LANES
    j = lax.broadcasted_iota(jnp.int32, (chunk, chunk), 0)
    t = lax.broadcasted_iota(jnp.int32, (chunk, chunk), 1)
    triu = (j <= t).astype(BF16)
    cs = jnp.concatenate([triu, triu], axis=0)
    row = pl.BlockSpec((None, ts, width), lambda bi, i: (bi, i, 0))
    return pl.pallas_call(
        functools.partial(_mlstm_kernel, ts=ts, chunk=chunk, d=d, nh=nh),
        grid=(b, s // ts),
        in_specs=[row, row, row,
                  pl.BlockSpec((None, width, ts), lambda bi, i: (bi, 0, i)),
                  pl.BlockSpec((None, 2 * nh, ts), lambda bi, i: (bi, 0, i)),
                  _const_spec((2 * nh, 1)),
                  _const_spec(cs.shape)],
        out_specs=row,
        out_shape=jax.ShapeDtypeStruct((b, s, width), F32),
        scratch_shapes=[pltpu.VMEM((nh, 2 * d, d), F32), pltpu.VMEM((nh, SUBLANES, LANES), F32)],
        compiler_params=pltpu.CompilerParams(
            dimension_semantics=("arbitrary", "arbitrary"), vmem_limit_bytes=VMEM_LIMIT_BYTES),
        name="mlstm",
    )(q, k, v, vt, gates, gate_bias, cs)


def _out_kernel(ysb_ref, gz_ref, hml_ref, go_ref, gmz_ref, x_ref, p_ref,
                sbw_ref, mlw_ref, gsb_ref, gml_ref, wout_ref, pw_ref, wgate_ref, bgate_ref, wup_ref,
                o_ref, *, half):
    ysb = ysb_ref[...]
    ms_sb = _dot((ysb * ysb).astype(BF16), gsb_ref[...])
    a = (ysb * lax.rsqrt(ms_sb + EPS)) * sbw_ref[...] * gz_ref[...].astype(F32)
    yml = go_ref[...].astype(F32) * hml_ref[...]
    ms_ml = _dot((yml * yml).astype(BF16), gml_ref[...])
    bb = (yml * lax.rsqrt(ms_ml + EPS)) * mlw_ref[...] * gmz_ref[...].astype(F32)
    y = _dot(a.astype(BF16), wout_ref[0:half, :]) + _dot(bb.astype(BF16), wout_ref[half:2 * half, :])
    ms_y = jnp.mean(y * y, axis=-1, keepdims=True)
    h = x_ref[...] + (y * lax.rsqrt(ms_y + EPS)) * pw_ref[...]
    gate = _sigmoid(_dot(h.astype(BF16), wgate_ref[...]) + bgate_ref[...])
    o_ref[...] = h + gate * _dot(p_ref[...].astype(BF16), wup_ref[...])


def _head_mean_matrix(width, heads):
    hd = width // heads
    j = lax.broadcasted_iota(jnp.int32, (width, width), 0) // hd
    t = lax.broadcasted_iota(jnp.int32, (width, width), 1) // hd
    return jnp.where(j == t, 1.0 / hd, 0.0).astype(BF16)


def _out_stage(ysb, gz, hml, go, gmz, x, p, sb_norm_w, ml_norm_w, w_out, post_norm_w,
               w_gate, b_gate, w_up, *, ts):
    t, d = x.shape
    half = d // 2
    pd = p.shape[-1]
    rows = lambda w: pl.BlockSpec((ts, w), lambda i: (i, 0))
    return pl.pallas_call(
        functools.partial(_out_kernel, half=half),
        grid=(t // ts,),
        in_specs=[rows(half), rows(half), rows(half), rows(half), rows(half), rows(d), rows(pd),
                  _const_spec((1, half)), _const_spec((1, half)),
                  _const_spec((half, half)), _const_spec((half, half)),
                  _const_spec((d, d)), _const_spec((1, d)),
                  _const_spec((d, d)), _const_spec((1, d)), _const_spec((pd, d))],
        out_specs=rows(d),
        out_shape=jax.ShapeDtypeStruct((t, d), F32),
        compiler_params=pltpu.CompilerParams(
            dimension_semantics=("arbitrary",), vmem_limit_bytes=VMEM_LIMIT_BYTES),
        name="out_stage",
    )(ysb, gz, hml, go, gmz, x, p,
      sb_norm_w.reshape(1, half), ml_norm_w.reshape(1, half),
      _head_mean_matrix(half, SB_HEADS), _head_mean_matrix(half, ML_HEADS),
      w_out, post_norm_w.reshape(1, d), w_gate, b_gate.reshape(1, d), w_up)


def _layer(h, p, pre_norm_w, w_in, ml_conv_w, ml_conv_b, ml_i_bias, ml_f_bias, sb_norm_w, ml_norm_w,
           w_out, post_norm_w, ple_w_up, ple_w_gate, ple_b_gate):
    b, s, d = h.shape
    half = d // 2
    tiles = _tiles(s)
    wb = w_in.astype(BF16)
    n_main = 9 * half
    q, kc, vc, gz, mq, mk, mv, mvt, go, gmz, gates = _inproj(
        h, pre_norm_w, wb[:, :n_main], wb[:, 6 * half:7 * half].T, wb[:, n_main:].T,
        ml_conv_w, ml_conv_b, ts=tiles["inproj"])
    ysb = _sb_attention(q, kc, vc, tq=tiles["sb_q"])
    gate_bias = jnp.concatenate([ml_i_bias, ml_f_bias]).reshape(2 * ML_HEADS, 1)
    hml = _mlstm(mq, mk, mv, mvt, gates, gate_bias, ts=tiles["ml"])
    flat = lambda a: a.reshape(b * s, a.shape[-1])
    out = _out_stage(flat(ysb), flat(gz), flat(hml), flat(go), flat(gmz), flat(h), flat(p),
                     sb_norm_w, ml_norm_w, w_out.astype(BF16), post_norm_w,
                     ple_w_gate.astype(BF16), ple_b_gate, ple_w_up.astype(BF16), ts=tiles["out"])
    return out.reshape(b, s, d)


def kernel(x, p, pre_norm_w, w_in, ml_conv_w, ml_conv_b, ml_i_bias, ml_f_bias, sb_norm_w, ml_norm_w,
           w_out, post_norm_w, ple_w_up, ple_w_gate, ple_b_gate):
    h = x
    for i in range(pre_norm_w.shape[0]):
        h = _layer(h, p[i], pre_norm_w[i], w_in[i], ml_conv_w[i], ml_conv_b[i], ml_i_bias[i], ml_f_bias[i],
                   sb_norm_w[i], ml_norm_w[i], w_out[i], post_norm_w[i], ple_w_up[i], ple_w_gate[i],
                   ple_b_gate[i])
    return h
```

```python
import functools

import jax
import jax.numpy as jnp
from jax import lax
from jax.experimental import pallas as pl
from jax.experimental.pallas import tpu as pltpu

SB_HEADS = 8
ML_HEADS = 4
CONV_K = 4
EPS = 1e-6
LOG2_E = 1.4426950408889634
SB_DEAD_CARRY = 152.0

LANES = 128
SUBLANES = 8
VMEM_LIMIT_BYTES = 56 * 1024 * 1024

F32 = jnp.float32
BF16 = jnp.bfloat16


def _tiles(seq):
    def pick(pref):
        t = min(pref, seq)
        assert seq % t == 0 and t % LANES == 0
        return t
    return dict(inproj=pick(512), sb_q=pick(512), sb_pairs=4, ml=pick(256), out=pick(512))


def _dot(a, b):
    return jnp.dot(a, b, preferred_element_type=F32)


def _dot_nt(a, b):
    return lax.dot_general(a, b, (((1,), (1,)), ((), ())), preferred_element_type=F32)


def _split_bf16(x):
    hi = x.astype(BF16)
    lo = (x - hi.astype(F32)).astype(BF16)
    return hi, lo


def _sigmoid(x):
    return 1.0 / (1.0 + jnp.exp(-x))


def _silu(x):
    return x * _sigmoid(x)


def _softplus(x):
    return jnp.maximum(x, 0.0) + jnp.log(1.0 + jnp.exp(-jnp.abs(x)))


def _log_sigmoid(x):
    return -_softplus(-x)


def _softplus2(x):
    return jnp.maximum(x, 0.0) + jnp.log2(1.0 + jnp.exp2(-jnp.abs(x)))


def _const_spec(shape):
    return pl.BlockSpec(shape, lambda *_: (0,) * len(shape), pipeline_mode=pl.Buffered(1))


def _store_pair_masked(dst_ref, a, ts):
    first_head = lax.broadcasted_iota(jnp.int32, (LANES, LANES), 1) < LANES // 2
    zero = jnp.zeros((LANES, LANES), a.dtype)
    for p in range(a.shape[1] // LANES):
        for j in range(ts // LANES):
            blk = a[j * LANES:(j + 1) * LANES, p * LANES:(p + 1) * LANES]
            dst_ref[p, j, 0:LANES, :] = jnp.where(first_head, blk, zero)
            dst_ref[p, j, LANES:2 * LANES, :] = jnp.where(first_head, zero, blk)


def _inproj_kernel(x_ref, pnw_ref, w_ref, wvt_ref, wg_ref, cw_ref, cb_ref,
                   q_ref, kc_ref, vc_ref, gz_ref, mq_ref, mk_ref, mv_ref, mvt_ref,
                   go_ref, gmz_ref, gates_ref, conv_buf, *, ts, half, sb_scale, ml_scale):
    x = x_ref[...]
    ms = jnp.mean(x * x, axis=-1, keepdims=True)
    u = (x * lax.rsqrt(ms + EPS)) * pnw_ref[...]
    ub = u.astype(BF16)

    def seg(j):
        return _dot(ub, w_ref[:, j * half:(j + 1) * half])

    q_ref[...] = (seg(0) * sb_scale).astype(BF16)
    _store_pair_masked(kc_ref, seg(1).astype(BF16), ts)
    _store_pair_masked(vc_ref, seg(2).astype(BF16), ts)
    gz_ref[...] = _silu(seg(3)).astype(BF16)

    @pl.when(pl.program_id(1) == 0)
    def _():
        conv_buf[0:SUBLANES, :] = jnp.zeros((SUBLANES, 2 * half), F32)

    conv_buf[SUBLANES:SUBLANES + ts, 0:half] = seg(4)
    conv_buf[SUBLANES:SUBLANES + ts, half:2 * half] = seg(5)
    acc = cb_ref[...]
    for j in range(CONV_K):
        acc = acc + cw_ref[j:j + 1, :] * conv_buf[pl.ds(SUBLANES - (CONV_K - 1) + j, ts), :]
    qk = _silu(acc)
    mq_ref[...] = qk[:, :half].astype(BF16)
    mk_ref[...] = (qk[:, half:] * ml_scale).astype(BF16)
    conv_buf[0:SUBLANES, :] = conv_buf[ts:ts + SUBLANES, :]

    mv_ref[...] = seg(6).astype(BF16)
    mvt_ref[...] = _dot_nt(wvt_ref[...], ub).astype(BF16)
    go_ref[...] = _sigmoid(seg(7)).astype(BF16)
    gmz_ref[...] = _silu(seg(8)).astype(BF16)
    gates_ref[...] = _dot_nt(wg_ref[...], ub)


def _inproj(x, pre_norm_w, w_main, w_vt, w_gates, conv_w, conv_b, *, ts):
    b, s, d = x.shape
    half = d // 2
    n_gates = w_gates.shape[0]
    sb_scale = float(half // SB_HEADS) ** -0.5 * LOG2_E
    ml_scale = float(half // ML_HEADS) ** -0.5
    row = pl.BlockSpec((None, ts, half), lambda bi, i: (bi, i, 0))
    act = jax.ShapeDtypeStruct((b, s, half), BF16)
    pairs = half // LANES
    pair_blocks = pl.BlockSpec((None, pairs, ts // LANES, 2 * LANES, LANES), lambda bi, i: (bi, 0, i, 0, 0))
    pair_act = jax.ShapeDtypeStruct((b, pairs, s // LANES, 2 * LANES, LANES), BF16)
    return pl.pallas_call(
        functools.partial(_inproj_kernel, ts=ts, half=half, sb_scale=sb_scale, ml_scale=ml_scale),
        grid=(b, s // ts),
        in_specs=[
            pl.BlockSpec((None, ts, d), lambda bi, i: (bi, i, 0)),
            _const_spec((1, d)),
            _const_spec(w_main.shape),
            _const_spec(w_vt.shape),
            _const_spec(w_gates.shape),
            _const_spec(conv_w.shape),
            _const_spec((1, 2 * half)),
        ],
        out_specs=[row, pair_blocks, pair_blocks, row, row, row, row,
                   pl.BlockSpec((None, half, ts), lambda bi, i: (bi, 0, i)),
                   row, row,
                   pl.BlockSpec((None, n_gates, ts), lambda bi, i: (bi, 0, i))],
        out_shape=[act, pair_act, pair_act, act, act, act, act,
                   jax.ShapeDtypeStruct((b, half, s), BF16),
                   act, act,
                   jax.ShapeDtypeStruct((b, n_gates, s), F32)],
        scratch_shapes=[pltpu.VMEM((ts + SUBLANES, 2 * half), F32)],
        compiler_params=pltpu.CompilerParams(
            dimension_semantics=("arbitrary", "arbitrary"), vmem_limit_bytes=VMEM_LIMIT_BYTES),
        name="inproj",
    )(x, pre_norm_w.reshape(1, d), w_main, w_vt, w_gates, conv_w, conv_b.reshape(1, 2 * half))


def _sb_kernel(q_ref, kc_ref, vc_ref, tt_ref, y_ref, acc_ref, carry_ref, *, tq, pairs):
    i = pl.program_id(2)
    nsub = tq // LANES
    acc_ref[...] = jnp.zeros_like(acc_ref)
    carry_ref[...] = jnp.zeros_like(carry_ref)

    def block(kb, r0, masked):
        rows = slice(r0, tq)
        if masked:
            n = tq - r0
            visible = (lax.broadcasted_iota(jnp.int32, (n, LANES), 1)
                       < lax.broadcasted_iota(jnp.int32, (n, LANES), 0))
        for p in range(pairs):
            lanes = slice(p * LANES, (p + 1) * LANES)
            z = _dot_nt(q_ref[rows, lanes], kc_ref[p, kb])
            a_parts = []
            for hh in range(2):
                zh = z[:, hh * LANES:(hh + 1) * LANES]
                sp = _softplus2(zh)
                if masked:
                    sp = jnp.where(visible, sp, 0.0)
                hi, lo = _split_bf16(sp)
                ct = _dot(jnp.concatenate([hi, lo], axis=1), tt_ref[...])
                carry = carry_ref[2 * p + hh, rows, :]
                a = jnp.exp2(zh - (ct[:, :LANES] + carry))
                if masked:
                    a = jnp.where(visible, a, 0.0)
                carry_ref[2 * p + hh, rows, :] = carry + ct[:, LANES:]
                a_parts.append(a.astype(BF16))
            acc_ref[rows, lanes] += _dot(jnp.concatenate(a_parts, axis=1), vc_ref[p, kb])

    for r in reversed(range(nsub)):
        block(i * nsub + r, r * LANES, True)

    def body(state):
        kb, _ = state
        block(kb, 0, False)
        return kb - 1, jnp.min(carry_ref[...])

    def live(state):
        kb, min_carry = state
        return jnp.logical_and(kb >= 0, min_carry < SB_DEAD_CARRY)

    lax.while_loop(live, body, (i * nsub - 1, jnp.float32(0.0)))
    y_ref[...] = acc_ref[...]


def _sb_attention(q, kc, vc, *, tq, pairs_per_step):
    b, s, width = q.shape
    pairs = width // LANES
    assert pairs % pairs_per_step == 0
    pw = pairs_per_step * LANES
    j = lax.broadcasted_iota(jnp.int32, (LANES, LANES), 0)
    t = lax.broadcasted_iota(jnp.int32, (LANES, LANES), 1)
    tri_ones = jnp.concatenate([(j >= t).astype(BF16), jnp.ones((LANES, LANES), BF16)], axis=1)
    tt = jnp.concatenate([tri_ones, tri_ones], axis=0)
    qspec = pl.BlockSpec((None, tq, pw), lambda bi, p, i: (bi, i, p))
    kvspec = pl.BlockSpec((None, pairs_per_step, s // LANES, 2 * LANES, LANES),
                          lambda bi, p, i: (bi, p, 0, 0, 0))
    return pl.pallas_call(
        functools.partial(_sb_kernel, tq=tq, pairs=pairs_per_step),
        grid=(b, pairs // pairs_per_step, s // tq),
        in_specs=[qspec, kvspec, kvspec, _const_spec(tt.shape)],
        out_specs=qspec,
        out_shape=jax.ShapeDtypeStruct((b, s, width), F32),
        scratch_shapes=[pltpu.VMEM((tq, pw), F32), pltpu.VMEM((2 * pairs_per_step, tq, LANES), F32)],
        compiler_params=pltpu.CompilerParams(
            dimension_semantics=("arbitrary", "arbitrary", "arbitrary"), vmem_limit_bytes=VMEM_LIMIT_BYTES),
        name="sb_attention",
    )(q, kc, vc, tt)


def _mlstm_kernel(q_ref, k_ref, v_ref, vt_ref, g_ref, gb_ref, cs_ref, h_ref, st_ref, m_ref,
                  *, ts, chunk, d, nh):
    @pl.when(pl.program_id(1) == 0)
    def _():
        st_ref[...] = jnp.zeros_like(st_ref)
        m_ref[...] = jnp.zeros_like(m_ref)

    r_i = lax.broadcasted_iota(jnp.int32, (chunk, chunk), 0)
    c_i = lax.broadcasted_iota(jnp.int32, (chunk, chunk), 1)
    lower = c_i <= r_i
    eye = c_i == r_i
    ones_cols = jnp.ones((2 * chunk, d), BF16)
    ones_v = jnp.ones((chunk, d), BF16)
    ones_vt = jnp.ones((d, chunk), BF16)

    def col_rep(r):
        hi, lo = _split_bf16(jnp.where(eye, r, 0.0))
        return _dot(jnp.concatenate([hi, lo], axis=1), ones_cols)

    for c in range(ts // chunk):
        sl = slice(c * chunk, (c + 1) * chunk)
        gates = g_ref[:, sl] + gb_ref[...]
        hi, lo = _split_bf16(_log_sigmoid(gates))
        bcum = _dot(jnp.concatenate([hi, lo], axis=1), cs_ref[...])
        for h in range(nh):
            hs = slice(h * d, (h + 1) * d)
            q = q_ref[sl, hs]
            k = k_ref[sl, hs]
            v = v_ref[sl, hs]
            vt = vt_ref[hs, sl]
            ig = gates[h:h + 1, :]
            b = bcum[nh + h:nh + h + 1, :]
            u = ig - b
            b_last = b[:, chunk - 1:chunk]
            m_loc = jnp.max(b_last + u, axis=-1, keepdims=True)
            b_rep = col_rep(b)
            u_rep = col_rep(u)
            m_prev = m_ref[h, 0:1, :]
            state = st_ref[h]

            w = jnp.where(lower, u, -jnp.inf)
            m_row = jnp.maximum(jnp.max(w, axis=-1, keepdims=True), m_prev[:, 0:1])
            scores = (_dot_nt(q, k) * jnp.exp(w - m_row)).astype(BF16)
            intra = _dot(scores, jnp.concatenate([v, ones_v], axis=1))
            inter = _dot_nt(q, state.astype(BF16))
            carry_scale = jnp.exp(m_prev - m_row)
            num = intra[:, :d] + carry_scale * inter[:, :d]
            den = intra[:, d:] + carry_scale * inter[:, d:]
            h_ref[sl, hs] = num / jnp.maximum(jnp.abs(den), jnp.exp(-b_rep - m_row))

            kw = (k.astype(F32) * jnp.exp(b_last + u_rep - m_loc)).astype(BF16)
            upd = _dot(jnp.concatenate([vt, ones_vt], axis=0), kw)
            m_new = jnp.maximum(b_last + m_prev, m_loc)
            st_ref[h] = jnp.exp(b_last + m_prev - m_new) * state + jnp.exp(m_loc - m_new) * upd
            m_ref[h, 0:1, :] = m_new


def _mlstm(q, k, v, vt, gates, gate_bias, *, ts, chunk=LANES):
    b, s, width = q.shape
    nh = ML_HEADS
    d = width // nh
    assert d == LANES and chunk == LANES
    j = lax.broadcasted_iota(jnp.int32, (chunk, chunk), 0)
    t = lax.broadcasted_iota(jnp.int32, (chunk, chunk), 1)
    triu = (j <= t).astype(BF16)
    cs = jnp.concatenate([triu, triu], axis=0)
    row = pl.BlockSpec((None, ts, width), lambda bi, i: (bi, i, 0))
    return pl.pallas_call(
        functools.partial(_mlstm_kernel, ts=ts, chunk=chunk, d=d, nh=nh),
        grid=(b, s // ts),
        in_specs=[row, row, row,
                  pl.BlockSpec((None, width, ts), lambda bi, i: (bi, 0, i)),
                  pl.BlockSpec((None, 2 * nh, ts), lambda bi, i: (bi, 0, i)),
                  _const_spec((2 * nh, 1)),
                  _const_spec(cs.shape)],
        out_specs=row,
        out_shape=jax.ShapeDtypeStruct((b, s, width), F32),
        scratch_shapes=[pltpu.VMEM((nh, 2 * d, d), F32), pltpu.VMEM((nh, SUBLANES, LANES), F32)],
        compiler_params=pltpu.CompilerParams(
            dimension_semantics=("arbitrary", "arbitrary"), vmem_limit_bytes=VMEM_LIMIT_BYTES),
        name="mlstm",
    )(q, k, v, vt, gates, gate_bias, cs)


def _out_kernel(ysb_ref, gz_ref, hml_ref, go_ref, gmz_ref, x_ref, p_ref,
                sbw_ref, mlw_ref, gsb_ref, gml_ref, wout_ref, pw_ref, wgate_ref, bgate_ref, wup_ref,
                o_ref, *, half):
    ysb = ysb_ref[...]
    ms_sb = _dot((ysb * ysb).astype(BF16), gsb_ref[...])
    a = (ysb * lax.rsqrt(ms_sb + EPS)) * sbw_ref[...] * gz_ref[...].astype(F32)
    yml = go_ref[...].astype(F32) * hml_ref[...]
    ms_ml = _dot((yml * yml).astype(BF16), gml_ref[...])
    bb = (yml * lax.rsqrt(ms_ml + EPS)) * mlw_ref[...] * gmz_ref[...].astype(F32)
    y = _dot(a.astype(BF16), wout_ref[0:half, :]) + _dot(bb.astype(BF16), wout_ref[half:2 * half, :])
    ms_y = jnp.mean(y * y, axis=-1, keepdims=True)
    h = x_ref[...] + (y * lax.rsqrt(ms_y + EPS)) * pw_ref[...]
    gate = _sigmoid(_dot(h.astype(BF16), wgate_ref[...]) + bgate_ref[...])
    o_ref[...] = h + gate * _dot(p_ref[...].astype(BF16), wup_ref[...])


def _head_mean_matrix(width, heads):
    hd = width // heads
    j = lax.broadcasted_iota(jnp.int32, (width, width), 0) // hd
    t = lax.broadcasted_iota(jnp.int32, (width, width), 1) // hd
    return jnp.where(j == t, 1.0 / hd, 0.0).astype(BF16)


def _out_stage(ysb, gz, hml, go, gmz, x, p, sb_norm_w, ml_norm_w, w_out, post_norm_w,
               w_gate, b_gate, w_up, *, ts):
    t, d = x.shape
    half = d // 2
    pd = p.shape[-1]
    rows = lambda w: pl.BlockSpec((ts, w), lambda i: (i, 0))
    return pl.pallas_call(
        functools.partial(_out_kernel, half=half),
        grid=(t // ts,),
        in_specs=[rows(half), rows(half), rows(half), rows(half), rows(half), rows(d), rows(pd),
                  _const_spec((1, half)), _const_spec((1, half)),
                  _const_spec((half, half)), _const_spec((half, half)),
                  _const_spec((d, d)), _const_spec((1, d)),
                  _const_spec((d, d)), _const_spec((1, d)), _const_spec((pd, d))],
        out_specs=rows(d),
        out_shape=jax.ShapeDtypeStruct((t, d), F32),
        compiler_params=pltpu.CompilerParams(
            dimension_semantics=("arbitrary",), vmem_limit_bytes=VMEM_LIMIT_BYTES),
        name="out_stage",
    )(ysb, gz, hml, go, gmz, x, p,
      sb_norm_w.reshape(1, half), ml_norm_w.reshape(1, half),
      _head_mean_matrix(half, SB_HEADS), _head_mean_matrix(half, ML_HEADS),
      w_out, post_norm_w.reshape(1, d), w_gate, b_gate.reshape(1, d), w_up)


def _layer(h, p, pre_norm_w, w_in, ml_conv_w, ml_conv_b, ml_i_bias, ml_f_bias, sb_norm_w, ml_norm_w,
           w_out, post_norm_w, ple_w_up, ple_w_gate, ple_b_gate):
    b, s, d = h.shape
    half = d // 2
    tiles = _tiles(s)
    wb = w_in.astype(BF16)
    n_main = 9 * half
    q, kc, vc, gz, mq, mk, mv, mvt, go, gmz, gates = _inproj(
        h, pre_norm_w, wb[:, :n_main], wb[:, 6 * half:7 * half].T, wb[:, n_main:].T,
        ml_conv_w, ml_conv_b, ts=tiles["inproj"])
    ysb = _sb_attention(q, kc, vc, tq=tiles["sb_q"], pairs_per_step=tiles["sb_pairs"])
    gate_bias = jnp.concatenate([ml_i_bias, ml_f_bias]).reshape(2 * ML_HEADS, 1)
    hml = _mlstm(mq, mk, mv, mvt, gates, gate_bias, ts=tiles["ml"])
    flat = lambda a: a.reshape(b * s, a.shape[-1])
    out = _out_stage(flat(ysb), flat(gz), flat(hml), flat(go), flat(gmz), flat(h), flat(p),
                     sb_norm_w, ml_norm_w, w_out.astype(BF16), post_norm_w,
                     ple_w_gate.astype(BF16), ple_b_gate, ple_w_up.astype(BF16), ts=tiles["out"])
    return out.reshape(b, s, d)


def kernel(x, p, pre_norm_w, w_in, ml_conv_w, ml_conv_b, ml_i_bias, ml_f_bias, sb_norm_w, ml_norm_w,
           w_out, post_norm_w, ple_w_up, ple_w_gate, ple_b_gate):
    h = x
    for i in range(pre_norm_w.shape[0]):
        h = _layer(h, p[i], pre_norm_w[i], w_in[i], ml_conv_w[i], ml_conv_b[i], ml_i_bias[i], ml_f_bias[i],
                   sb_norm_w[i], ml_norm_w[i], w_out[i], post_norm_w[i], ple_w_up[i], ple_w_gate[i],
                   ple_b_gate[i])
    return h
```

```python
import functools

import jax
import jax.numpy as jnp
from jax import lax
from jax.experimental import pallas as pl
from jax.experimental.pallas import tpu as pltpu

SB_HEADS = 8
ML_HEADS = 4
CONV_K = 4
EPS = 1e-6
LOG2_E = 1.4426950408889634
SB_DEAD_CARRY = 152.0

LANES = 128
SUBLANES = 8
MXU_WIDTH = 256
VMEM_LIMIT_BYTES = 56 * 1024 * 1024

F32 = jnp.float32
BF16 = jnp.bfloat16


def _tiles(seq):
    def pick(pref):
        t = min(pref, seq)
        assert seq % t == 0 and t % LANES == 0
        return t
    return dict(inproj=pick(512), sb_q=pick(512), sb_pairs=4, ml=pick(1024), out=pick(512))


def _dot(a, b):
    return jnp.dot(a, b, preferred_element_type=F32)


def _dot_nt(a, b):
    return lax.dot_general(a, b, (((1,), (1,)), ((), ())), preferred_element_type=F32)


def _split_bf16(x):
    hi = x.astype(BF16)
    lo = (x - hi.astype(F32)).astype(BF16)
    return hi, lo


def _sigmoid(x):
    return 1.0 / (1.0 + jnp.exp(-x))


def _silu(x):
    return x * _sigmoid(x)


def _softplus(x):
    return jnp.maximum(x, 0.0) + jnp.log(1.0 + jnp.exp(-jnp.abs(x)))


def _log_sigmoid(x):
    return -_softplus(-x)


def _softplus2(x):
    return jnp.maximum(x, 0.0) + jnp.log2(1.0 + jnp.exp2(-jnp.abs(x)))


def _const_spec(shape):
    return pl.BlockSpec(shape, lambda *_: (0,) * len(shape), pipeline_mode=pl.Buffered(1))


def _store_pair_masked(dst_ref, a, ts, pair0):
    first_head = lax.broadcasted_iota(jnp.int32, (LANES, LANES), 1) < LANES // 2
    zero = jnp.zeros((LANES, LANES), a.dtype)
    for p in range(a.shape[1] // LANES):
        for j in range(ts // LANES):
            blk = a[j * LANES:(j + 1) * LANES, p * LANES:(p + 1) * LANES]
            dst_ref[pair0 + p, j, 0:LANES, :] = jnp.where(first_head, blk, zero)
            dst_ref[pair0 + p, j, LANES:2 * LANES, :] = jnp.where(first_head, zero, blk)


def _inproj_kernel(x_ref, pnw_ref, w_ref, wvt_ref, wg_ref, cw_ref, cb_ref,
                   q_ref, kc_ref, vc_ref, gz_ref, mq_ref, mk_ref, mv_ref, mvt_ref,
                   go_ref, gmz_ref, gates_ref, conv_buf, u_ref, *, ts, half, sb_scale, ml_scale):
    @pl.when(pl.program_id(1) == 0)
    def _():
        conv_buf[0:SUBLANES, :] = jnp.zeros((SUBLANES, 2 * half), F32)

    x = x_ref[...]
    ms = jnp.mean(x * x, axis=-1, keepdims=True)
    u_ref[...] = ((x * lax.rsqrt(ms + EPS)) * pnw_ref[...]).astype(BF16)

    cw = MXU_WIDTH
    per_seg = half // cw

    def proj(seg, sub):
        col = seg * half + sub * cw
        return _dot(u_ref[...], w_ref[:, col:col + cw])

    def conv_silu(seg, sub):
        cols = slice((seg - 4) * half + sub * cw, (seg - 4) * half + (sub + 1) * cw)
        conv_buf[SUBLANES:SUBLANES + ts, cols] = proj(seg, sub)
        acc = cb_ref[:, cols]
        for j in range(CONV_K):
            acc = acc + cw_ref[j:j + 1, cols] * conv_buf[pl.ds(SUBLANES - (CONV_K - 1) + j, ts), cols]
        conv_buf[0:SUBLANES, cols] = conv_buf[ts:ts + SUBLANES, cols]
        return _silu(acc)

    for sub in range(per_seg):
        out = slice(sub * cw, (sub + 1) * cw)
        mq_ref[:, out] = conv_silu(4, sub).astype(BF16)
        mk_ref[:, out] = (conv_silu(5, sub) * ml_scale).astype(BF16)
        q_ref[:, out] = (proj(0, sub) * sb_scale).astype(BF16)
        _store_pair_masked(kc_ref, proj(1, sub).astype(BF16), ts, sub * (cw // LANES))
        _store_pair_masked(vc_ref, proj(2, sub).astype(BF16), ts, sub * (cw // LANES))
        gz_ref[:, out] = _silu(proj(3, sub)).astype(BF16)
        mv_ref[:, out] = proj(6, sub).astype(BF16)
        go_ref[:, out] = _sigmoid(proj(7, sub)).astype(BF16)
        gmz_ref[:, out] = _silu(proj(8, sub)).astype(BF16)
    mvt_ref[...] = _dot_nt(wvt_ref[...], u_ref[...]).astype(BF16)
    gates_ref[...] = _dot_nt(wg_ref[...], u_ref[...])


def _inproj(x, pre_norm_w, w_main, w_vt, w_gates, conv_w, conv_b, *, ts):
    b, s, d = x.shape
    half = d // 2
    n_gates = w_gates.shape[0]
    sb_scale = float(half // SB_HEADS) ** -0.5 * LOG2_E
    ml_scale = float(half // ML_HEADS) ** -0.5
    row = pl.BlockSpec((None, ts, half), lambda bi, i: (bi, i, 0))
    act = jax.ShapeDtypeStruct((b, s, half), BF16)
    pairs = half // LANES
    pair_blocks = pl.BlockSpec((None, pairs, ts // LANES, 2 * LANES, LANES), lambda bi, i: (bi, 0, i, 0, 0))
    pair_act = jax.ShapeDtypeStruct((b, pairs, s // LANES, 2 * LANES, LANES), BF16)
    return pl.pallas_call(
        functools.partial(_inproj_kernel, ts=ts, half=half, sb_scale=sb_scale, ml_scale=ml_scale),
        grid=(b, s // ts),
        in_specs=[
            pl.BlockSpec((None, ts, d), lambda bi, i: (bi, i, 0)),
            _const_spec((1, d)),
            _const_spec(w_main.shape),
            _const_spec(w_vt.shape),
            _const_spec(w_gates.shape),
            _const_spec(conv_w.shape),
            _const_spec((1, 2 * half)),
        ],
        out_specs=[row, pair_blocks, pair_blocks, row, row, row, row,
                   pl.BlockSpec((None, half, ts), lambda bi, i: (bi, 0, i)),
                   row, row,
                   pl.BlockSpec((None, n_gates, ts), lambda bi, i: (bi, 0, i))],
        out_shape=[act, pair_act, pair_act, act, act, act, act,
                   jax.ShapeDtypeStruct((b, half, s), BF16),
                   act, act,
                   jax.ShapeDtypeStruct((b, n_gates, s), F32)],
        scratch_shapes=[pltpu.VMEM((ts + SUBLANES, 2 * half), F32), pltpu.VMEM((ts, d), BF16)],
        compiler_params=pltpu.CompilerParams(
            dimension_semantics=("arbitrary", "arbitrary"), vmem_limit_bytes=VMEM_LIMIT_BYTES),
        name="inproj",
    )(x, pre_norm_w.reshape(1, d), w_main, w_vt, w_gates, conv_w, conv_b.reshape(1, 2 * half))


def _sb_kernel(q_ref, kc_ref, vc_ref, tt_ref, y_ref, acc_ref, carry_ref, *, tq, pairs):
    i = pl.program_id(2)
    nsub = tq // LANES
    acc_ref[...] = jnp.zeros_like(acc_ref)
    carry_ref[...] = jnp.zeros_like(carry_ref)

    def block(kb, r0, masked):
        rows = slice(r0, tq)
        if masked:
            n = tq - r0
            visible = (lax.broadcasted_iota(jnp.int32, (n, LANES), 1)
                       < lax.broadcasted_iota(jnp.int32, (n, LANES), 0))
        for p in range(pairs):
            lanes = slice(p * LANES, (p + 1) * LANES)
            z = _dot_nt(q_ref[rows, lanes], kc_ref[p, kb])
            a_parts = []
            for hh in range(2):
                zh = z[:, hh * LANES:(hh + 1) * LANES]
                sp = _softplus2(zh)
                if masked:
                    sp = jnp.where(visible, sp, 0.0)
                hi, lo = _split_bf16(sp)
                ct = _dot(jnp.concatenate([hi, lo], axis=1), tt_ref[...])
                carry = carry_ref[2 * p + hh, rows, :]
                a = jnp.exp2(zh - (ct[:, :LANES] + carry))
                if masked:
                    a = jnp.where(visible, a, 0.0)
                carry_ref[2 * p + hh, rows, :] = carry + ct[:, LANES:]
                a_parts.append(a.astype(BF16))
            acc_ref[rows, lanes] += _dot(jnp.concatenate(a_parts, axis=1), vc_ref[p, kb])

    for r in reversed(range(nsub)):
        block(i * nsub + r, r * LANES, True)

    def body(state):
        kb, _ = state
        block(kb, 0, False)
        return kb - 1, jnp.min(carry_ref[...])

    def live(state):
        kb, min_carry = state
        return jnp.logical_and(kb >= 0, min_carry < SB_DEAD_CARRY)

    lax.while_loop(live, body, (i * nsub - 1, jnp.float32(0.0)))
    y_ref[...] = acc_ref[...]


def _sb_attention(q, kc, vc, *, tq, pairs_per_step):
    b, s, width = q.shape
    pairs = width // LANES
    assert pairs % pairs_per_step == 0
    pw = pairs_per_step * LANES
    j = lax.broadcasted_iota(jnp.int32, (LANES, LANES), 0)
    t = lax.broadcasted_iota(jnp.int32, (LANES, LANES), 1)
    tri_ones = jnp.concatenate([(j >= t).astype(BF16), jnp.ones((LANES, LANES), BF16)], axis=1)
    tt = jnp.concatenate([tri_ones, tri_ones], axis=0)
    qspec = pl.BlockSpec((None, tq, pw), lambda bi, p, i: (bi, i, p))
    kvspec = pl.BlockSpec((None, pairs_per_step, s // LANES, 2 * LANES, LANES),
                          lambda bi, p, i: (bi, p, 0, 0, 0))
    return pl.pallas_call(
        functools.partial(_sb_kernel, tq=tq, pairs=pairs_per_step),
        grid=(b, pairs // pairs_per_step, s // tq),
        in_specs=[qspec, kvspec, kvspec, _const_spec(tt.shape)],
        out_specs=qspec,
        out_shape=jax.ShapeDtypeStruct((b, s, width), F32),
        scratch_shapes=[pltpu.VMEM((tq, pw), F32), pltpu.VMEM((2 * pairs_per_step, tq, LANES), F32)],
        compiler_params=pltpu.CompilerParams(
            dimension_semantics=("arbitrary", "arbitrary", "arbitrary"), vmem_limit_bytes=VMEM_LIMIT_BYTES),
        name="sb_attention",
    )(q, kc, vc, tt)


def _mlstm_kernel(q_ref, k_ref, v_ref, vt_ref, g_ref, gb_ref, cs_ref, h_ref, st_ref, m_ref,
                  *, ts, chunk, d, nh):
    @pl.when(pl.program_id(1) == 0)
    def _():
        st_ref[...] = jnp.zeros_like(st_ref)
        m_ref[...] = jnp.zeros_like(m_ref)

    r_i = lax.broadcasted_iota(jnp.int32, (chunk, chunk), 0)
    c_i = lax.broadcasted_iota(jnp.int32, (chunk, chunk), 1)
    lower = c_i <= r_i
    ones_v = jnp.ones((chunk, d), BF16)

    states = [st_ref[h] for h in range(nh)]
    m_run = [m_ref[h, 0:1, :] for h in range(nh)]
    for c in range(ts // chunk):
        sl = slice(c * chunk, (c + 1) * chunk)
        gates = g_ref[:, sl] + gb_ref[...]
        lf_all = _log_sigmoid(gates)
        hi, lo = _split_bf16(lf_all)
        bcum = _dot(jnp.concatenate([hi, lo], axis=1), cs_ref[...])
        for h in range(nh):
            hs = slice(h * d, (h + 1) * d)
            q = q_ref[sl, hs]
            k = k_ref[sl, hs]
            v = v_ref[sl, hs]
            vt = vt_ref[hs, sl]
            ig = gates[h:h + 1, :]
            lf = lf_all[nh + h:nh + h + 1, :]
            b = bcum[nh + h:nh + h + 1, :]
            u = ig - b
            b_last = b[:, chunk - 1:chunk]
            m_loc = jnp.max(b_last + u, axis=-1, keepdims=True)
            m_prev = m_run[h]
            state = states[h]

            b_col = jnp.sum(jnp.where(lower, lf, 0.0), axis=-1, keepdims=True)
            w = jnp.where(lower, u, -jnp.inf)
            m_row = jnp.maximum(jnp.max(w, axis=-1, keepdims=True), m_prev[:, 0:1])
            scores = (_dot_nt(q, k) * jnp.exp(w - m_row)).astype(BF16)
            intra = _dot(scores, jnp.concatenate([v, ones_v], axis=1))
            inter = _dot_nt(q, state.astype(BF16))
            carry_scale = jnp.exp(m_prev[:, 0:1] - m_row)
            num = intra[:, :d] + carry_scale * inter[:, :d]
            den = intra[:, d:] + carry_scale * inter[:, d:]
            h_ref[sl, hs] = num / jnp.maximum(jnp.abs(den), jnp.exp(-(b_col + m_row)))

            w_loc = jnp.exp(b_last + u - m_loc)
            lhs = jnp.concatenate([vt.astype(F32) * w_loc, jnp.broadcast_to(w_loc, (d, chunk))], axis=0)
            upd = _dot(lhs.astype(BF16), k)
            m_new = jnp.maximum(b_last + m_prev, m_loc)
            states[h] = jnp.exp(b_last + m_prev - m_new) * state + jnp.exp(m_loc - m_new) * upd
            m_run[h] = m_new
    for h in range(nh):
        st_ref[h] = states[h]
        m_ref[h, 0:1, :] = m_run[h]


def _mlstm(q, k, v, vt, gates, gate_bias, *, ts, chunk=LANES):
    b, s, width = q.shape
    nh = ML_HEADS
    d = width // nh
    assert d == LANES and chunk == LANES
    j = lax.broadcasted_iota(jnp.int32, (chunk, chunk), 0)
    t = lax.broadcasted_iota(jnp.int32, (chunk, chunk), 1)
    triu = (j <= t).astype(BF16)
    cs = jnp.concatenate([triu, triu], axis=0)
    row = pl.BlockSpec((None, ts, width), lambda bi, i: (bi, i, 0))
    return pl.pallas_call(
        functools.partial(_mlstm_kernel, ts=ts, chunk=chunk, d=d, nh=nh),
        grid=(b, s // ts),
        in_specs=[row, row, row,
                  pl.BlockSpec((None, width, ts), lambda bi, i: (bi, 0, i)),
                  pl.BlockSpec((None, 2 * nh, ts), lambda bi, i: (bi, 0, i)),
                  _const_spec((2 * nh, 1)),
                  _const_spec(cs.shape)],
        out_specs=row,
        out_shape=jax.ShapeDtypeStruct((b, s, width), F32),
        scratch_shapes=[pltpu.VMEM((nh, 2 * d, d), F32), pltpu.VMEM((nh, SUBLANES, LANES), F32)],
        compiler_params=pltpu.CompilerParams(
            dimension_semantics=("arbitrary", "arbitrary"), vmem_limit_bytes=VMEM_LIMIT_BYTES),
        name="mlstm",
    )(q, k, v, vt, gates, gate_bias, cs)


def _out_kernel(ysb_ref, gz_ref, hml_ref, go_ref, gmz_ref, x_ref, p_ref,
                sbw_ref, mlw_ref, gsb_ref, gml_ref, wout_ref, pw_ref, wgate_ref, bgate_ref, wup_ref,
                o_ref, *, half):
    ysb = ysb_ref[...]
    ms_sb = _dot((ysb * ysb).astype(BF16), gsb_ref[...])
    a = (ysb * lax.rsqrt(ms_sb + EPS)) * sbw_ref[...] * gz_ref[...].astype(F32)
    yml = go_ref[...].astype(F32) * hml_ref[...]
    ms_ml = _dot((yml * yml).astype(BF16), gml_ref[...])
    bb = (yml * lax.rsqrt(ms_ml + EPS)) * mlw_ref[...] * gmz_ref[...].astype(F32)
    y = _dot(a.astype(BF16), wout_ref[0:half, :]) + _dot(bb.astype(BF16), wout_ref[half:2 * half, :])
    ms_y = jnp.mean(y * y, axis=-1, keepdims=True)
    h = x_ref[...] + (y * lax.rsqrt(ms_y + EPS)) * pw_ref[...]
    gate = _sigmoid(_dot(h.astype(BF16), wgate_ref[...]) + bgate_ref[...])
    o_ref[...] = h + gate * _dot(p_ref[...].astype(BF16), wup_ref[...])


def _head_mean_matrix(width, heads):
    hd = width // heads
    j = lax.broadcasted_iota(jnp.int32, (width, width), 0) // hd
    t = lax.broadcasted_iota(jnp.int32, (width, width), 1) // hd
    return jnp.where(j == t, 1.0 / hd, 0.0).astype(BF16)


def _out_stage(ysb, gz, hml, go, gmz, x, p, sb_norm_w, ml_norm_w, w_out, post_norm_w,
               w_gate, b_gate, w_up, *, ts):
    t, d = x.shape
    half = d // 2
    pd = p.shape[-1]
    rows = lambda w: pl.BlockSpec((ts, w), lambda i: (i, 0))
    return pl.pallas_call(
        functools.partial(_out_kernel, half=half),
        grid=(t // ts,),
        in_specs=[rows(half), rows(half), rows(half), rows(half), rows(half), rows(d), rows(pd),
                  _const_spec((1, half)), _const_spec((1, half)),
                  _const_spec((half, half)), _const_spec((half, half)),
                  _const_spec((d, d)), _const_spec((1, d)),
                  _const_spec((d, d)), _const_spec((1, d)), _const_spec((pd, d))],
        out_specs=rows(d),
        out_shape=jax.ShapeDtypeStruct((t, d), F32),
        compiler_params=pltpu.CompilerParams(
            dimension_semantics=("arbitrary",), vmem_limit_bytes=VMEM_LIMIT_BYTES),
        name="out_stage",
    )(ysb, gz, hml, go, gmz, x, p,
      sb_norm_w.reshape(1, half), ml_norm_w.reshape(1, half),
      _head_mean_matrix(half, SB_HEADS), _head_mean_matrix(half, ML_HEADS),
      w_out, post_norm_w.reshape(1, d), w_gate, b_gate.reshape(1, d), w_up)


def _layer(h, p, pre_norm_w, w_in, ml_conv_w, ml_conv_b, ml_i_bias, ml_f_bias, sb_norm_w, ml_norm_w,
           w_out, post_norm_w, ple_w_up, ple_w_gate, ple_b_gate):
    b, s, d = h.shape
    half = d // 2
    tiles = _tiles(s)
    wb = w_in.astype(BF16)
    n_main = 9 * half
    q, kc, vc, gz, mq, mk, mv, mvt, go, gmz, gates = _inproj(
        h, pre_norm_w, wb[:, :n_main], wb[:, 6 * half:7 * half].T, wb[:, n_main:].T,
        ml_conv_w, ml_conv_b, ts=tiles["inproj"])
    ysb = _sb_attention(q, kc, vc, tq=tiles["sb_q"], pairs_per_step=tiles["sb_pairs"])
    gate_bias = jnp.concatenate([ml_i_bias, ml_f_bias]).reshape(2 * ML_HEADS, 1)
    hml = _mlstm(mq, mk, mv, mvt, gates, gate_bias, ts=tiles["ml"])
    flat = lambda a: a.reshape(b * s, a.shape[-1])
    out = _out_stage(flat(ysb), flat(gz), flat(hml), flat(go), flat(gmz), flat(h), flat(p),
                     sb_norm_w, ml_norm_w, w_out.astype(BF16), post_norm_w,
                     ple_w_gate.astype(BF16), ple_b_gate, ple_w_up.astype(BF16), ts=tiles["out"])
    return out.reshape(b, s, d)


def kernel(x, p, pre_norm_w, w_in, ml_conv_w, ml_conv_b, ml_i_bias, ml_f_bias, sb_norm_w, ml_norm_w,
           w_out, post_norm_w, ple_w_up, ple_w_gate, ple_b_gate):
    h = x
    for i in range(pre_norm_w.shape[0]):
        h = _layer(h, p[i], pre_norm_w[i], w_in[i], ml_conv_w[i], ml_conv_b[i], ml_i_bias[i], ml_f_bias[i],
                   sb_norm_w[i], ml_norm_w[i], w_out[i], post_norm_w[i], ple_w_up[i], ple_w_gate[i],
                   ple_b_gate[i])
    return h
```

```python
import functools

import jax
import jax.numpy as jnp
from jax import lax
from jax.experimental import pallas as pl
from jax.experimental.pallas import tpu as pltpu

SB_HEADS = 8
ML_HEADS = 4
CONV_K = 4
EPS = 1e-6
LOG2_E = 1.4426950408889634
SB_DEAD_CARRY = 152.0
SB_WINDOW = 3
SB_SWEEP_UNROLL = 2

LANES = 128
SUBLANES = 8
MXU_WIDTH = 256
VMEM_LIMIT_BYTES = 56 * 1024 * 1024

F32 = jnp.float32
BF16 = jnp.bfloat16


def _tiles(seq):
    def pick(pref):
        t = min(pref, seq)
        assert seq % t == 0 and t % LANES == 0
        return t
    return dict(inproj=pick(512), sb_pairs=2, ml=pick(1024), out=pick(512))


def _dot(a, b):
    return jnp.dot(a, b, preferred_element_type=F32)


def _dot_nt(a, b):
    return lax.dot_general(a, b, (((1,), (1,)), ((), ())), preferred_element_type=F32)


def _split_bf16(x):
    hi = x.astype(BF16)
    lo = (x - hi.astype(F32)).astype(BF16)
    return hi, lo


def _sigmoid(x):
    return 1.0 / (1.0 + jnp.exp(-x))


def _silu(x):
    return x * _sigmoid(x)


def _softplus(x):
    return jnp.maximum(x, 0.0) + jnp.log(1.0 + jnp.exp(-jnp.abs(x)))


def _log_sigmoid(x):
    return -_softplus(-x)


def _softplus2(x):
    return jnp.maximum(x, 0.0) + jnp.log2(1.0 + jnp.exp2(-jnp.abs(x)))


def _const_spec(shape):
    return pl.BlockSpec(shape, lambda *_: (0,) * len(shape), pipeline_mode=pl.Buffered(1))


def _store_pair_masked(dst_ref, a, ts, pair0):
    first_head = lax.broadcasted_iota(jnp.int32, (LANES, LANES), 1) < LANES // 2
    zero = jnp.zeros((LANES, LANES), a.dtype)
    for p in range(a.shape[1] // LANES):
        for j in range(ts // LANES):
            blk = a[j * LANES:(j + 1) * LANES, p * LANES:(p + 1) * LANES]
            dst_ref[pair0 + p, j, 0:LANES, :] = jnp.where(first_head, blk, zero)
            dst_ref[pair0 + p, j, LANES:2 * LANES, :] = jnp.where(first_head, zero, blk)


def _inproj_kernel(x_ref, pnw_ref, w_ref, wvt_ref, wg_ref, cw_ref, cb_ref,
                   q_ref, kc_ref, vc_ref, gz_ref, mq_ref, mk_ref, mv_ref, mvt_ref,
                   go_ref, gmz_ref, gates_ref, conv_buf, u_ref, *, ts, half, sb_scale, ml_scale):
    @pl.when(pl.program_id(1) == 0)
    def _():
        conv_buf[0:SUBLANES, :] = jnp.zeros((SUBLANES, 2 * half), F32)

    x = x_ref[...]
    ms = jnp.mean(x * x, axis=-1, keepdims=True)
    u_ref[...] = ((x * lax.rsqrt(ms + EPS)) * pnw_ref[...]).astype(BF16)

    cw = MXU_WIDTH
    per_seg = half // cw

    def proj(seg, sub):
        col = seg * half + sub * cw
        return _dot(u_ref[...], w_ref[:, col:col + cw])

    def conv_silu(seg, sub):
        cols = slice((seg - 4) * half + sub * cw, (seg - 4) * half + (sub + 1) * cw)
        conv_buf[SUBLANES:SUBLANES + ts, cols] = proj(seg, sub)
        acc = cb_ref[:, cols]
        for j in range(CONV_K):
            acc = acc + cw_ref[j:j + 1, cols] * conv_buf[pl.ds(SUBLANES - (CONV_K - 1) + j, ts), cols]
        conv_buf[0:SUBLANES, cols] = conv_buf[ts:ts + SUBLANES, cols]
        return _silu(acc)

    for sub in range(per_seg):
        out = slice(sub * cw, (sub + 1) * cw)
        mq_ref[:, out] = conv_silu(4, sub).astype(BF16)
        mk_ref[:, out] = (conv_silu(5, sub) * ml_scale).astype(BF16)
        q_ref[:, out] = (proj(0, sub) * sb_scale).astype(BF16)
        _store_pair_masked(kc_ref, proj(1, sub).astype(BF16), ts, sub * (cw // LANES))
        _store_pair_masked(vc_ref, proj(2, sub).astype(BF16), ts, sub * (cw // LANES))
        gz_ref[:, out] = _silu(proj(3, sub)).astype(BF16)
        mv_ref[:, out] = proj(6, sub).astype(BF16)
        go_ref[:, out] = _sigmoid(proj(7, sub)).astype(BF16)
        gmz_ref[:, out] = _silu(proj(8, sub)).astype(BF16)
    mvt_ref[...] = _dot_nt(wvt_ref[...], u_ref[...]).astype(BF16)
    gates_ref[...] = _dot_nt(wg_ref[...], u_ref[...])


def _inproj(x, pre_norm_w, w_main, w_vt, w_gates, conv_w, conv_b, *, ts):
    b, s, d = x.shape
    half = d // 2
    n_gates = w_gates.shape[0]
    sb_scale = float(half // SB_HEADS) ** -0.5 * LOG2_E
    ml_scale = float(half // ML_HEADS) ** -0.5
    row = pl.BlockSpec((None, ts, half), lambda bi, i: (bi, i, 0))
    act = jax.ShapeDtypeStruct((b, s, half), BF16)
    pairs = half // LANES
    pair_blocks = pl.BlockSpec((None, pairs, ts // LANES, 2 * LANES, LANES), lambda bi, i: (bi, 0, i, 0, 0))
    pair_act = jax.ShapeDtypeStruct((b, pairs, s // LANES, 2 * LANES, LANES), BF16)
    return pl.pallas_call(
        functools.partial(_inproj_kernel, ts=ts, half=half, sb_scale=sb_scale, ml_scale=ml_scale),
        grid=(b, s // ts),
        in_specs=[
            pl.BlockSpec((None, ts, d), lambda bi, i: (bi, i, 0)),
            _const_spec((1, d)),
            _const_spec(w_main.shape),
            _const_spec(w_vt.shape),
            _const_spec(w_gates.shape),
            _const_spec(conv_w.shape),
            _const_spec((1, 2 * half)),
        ],
        out_specs=[row, pair_blocks, pair_blocks, row, row, row, row,
                   pl.BlockSpec((None, half, ts), lambda bi, i: (bi, 0, i)),
                   row, row,
                   pl.BlockSpec((None, n_gates, ts), lambda bi, i: (bi, 0, i))],
        out_shape=[act, pair_act, pair_act, act, act, act, act,
                   jax.ShapeDtypeStruct((b, half, s), BF16),
                   act, act,
                   jax.ShapeDtypeStruct((b, n_gates, s), F32)],
        scratch_shapes=[pltpu.VMEM((ts + SUBLANES, 2 * half), F32), pltpu.VMEM((ts, d), BF16)],
        compiler_params=pltpu.CompilerParams(
            dimension_semantics=("arbitrary", "arbitrary"), vmem_limit_bytes=VMEM_LIMIT_BYTES),
        name="inproj",
    )(x, pre_norm_w.reshape(1, d), w_main, w_vt, w_gates, conv_w, conv_b.reshape(1, 2 * half))


def _sb_kernel(q_ref, kc_ref, vc_ref, tt_ref, y_ref, carry_ref, exit_min_ref, *, seq, pairs, unroll):
    nb = seq // LANES
    y_ref[...] = jnp.zeros_like(y_ref)
    carry_ref[...] = jnp.zeros_like(carry_ref)
    visible = (lax.broadcasted_iota(jnp.int32, (SB_WINDOW * LANES, LANES), 1)
               < lax.broadcasted_iota(jnp.int32, (SB_WINDOW * LANES, LANES), 0))

    def block(kb, sub0, n_sub, diagonal):
        rows = pl.ds(pl.multiple_of(sub0 * LANES, LANES), n_sub * LANES)

        def mask_diag(x):
            return jnp.where(visible[:n_sub * LANES], x, 0.0) if diagonal else x

        for p in range(pairs):
            lanes = slice(p * LANES, (p + 1) * LANES)
            z = _dot_nt(q_ref[rows, lanes], kc_ref[p, kb])
            a_parts = []
            for hh in range(2):
                zh = z[:, hh * LANES:(hh + 1) * LANES]
                hi, lo = _split_bf16(mask_diag(_softplus2(zh)))
                ct = _dot(jnp.concatenate([hi, lo], axis=1), tt_ref[...])
                carry = carry_ref[2 * p + hh, rows, :]
                a = mask_diag(jnp.exp2(zh - (ct[:, :LANES] + carry)))
                carry_ref[2 * p + hh, rows, :] = carry + ct[:, LANES:]
                a_parts.append(a.astype(BF16))
            y_ref[rows, lanes] += _dot(jnp.concatenate(a_parts, axis=1), vc_ref[p, kb])

    def min_carry(sub):
        rows = pl.ds(pl.multiple_of(sub * LANES, LANES), LANES)
        m = carry_ref[0, rows, :]
        for hh in range(1, 2 * pairs):
            m = jnp.minimum(m, carry_ref[hh, rows, :])
        return jnp.min(m, axis=0, keepdims=True)

    for back in range(1, SB_WINDOW):
        block(nb - back, nb - back, back, True)

    def sweep(j, carry):
        for step in range(unroll):
            kb = nb - SB_WINDOW - (j * unroll + step)
            block(kb, kb, SB_WINDOW, True)
            leaving = kb + SB_WINDOW - 1
            exit_min_ref[leaving] = jnp.broadcast_to(min_carry(leaving), (SUBLANES, LANES))
        return carry

    lax.fori_loop(0, (nb - SB_WINDOW + 1) // unroll, sweep, 0)

    def finish(sub, carry):
        def body(state):
            kb, _ = state
            block(kb, sub, 1, False)
            return kb - 1, jnp.min(min_carry(sub))

        def live(state):
            kb, m = state
            return jnp.logical_and(kb >= 0, m < SB_DEAD_CARRY)

        lax.while_loop(live, body, (sub - SB_WINDOW, jnp.min(exit_min_ref[sub])))
        return carry

    @pl.when(jnp.min(exit_min_ref[SB_WINDOW:nb]) < SB_DEAD_CARRY)
    def _():
        lax.fori_loop(SB_WINDOW, nb, finish, 0)


def _sb_attention(q, kc, vc, *, pairs_per_step):
    b, s, width = q.shape
    pairs = width // LANES
    nb = s // LANES
    assert pairs % pairs_per_step == 0 and nb > SB_WINDOW
    sweep_blocks = nb - SB_WINDOW + 1
    unroll = max(u for u in range(1, SB_SWEEP_UNROLL + 1) if sweep_blocks % u == 0)
    pw = pairs_per_step * LANES
    j = lax.broadcasted_iota(jnp.int32, (LANES, LANES), 0)
    t = lax.broadcasted_iota(jnp.int32, (LANES, LANES), 1)
    tri_ones = jnp.concatenate([(j >= t).astype(BF16), jnp.ones((LANES, LANES), BF16)], axis=1)
    tt = jnp.concatenate([tri_ones, tri_ones], axis=0)
    qspec = pl.BlockSpec((None, s, pw), lambda bi, p: (bi, 0, p))
    kvspec = pl.BlockSpec((None, pairs_per_step, nb, 2 * LANES, LANES), lambda bi, p: (bi, p, 0, 0, 0))
    return pl.pallas_call(
        functools.partial(_sb_kernel, seq=s, pairs=pairs_per_step, unroll=unroll),
        grid=(b, pairs // pairs_per_step),
        in_specs=[qspec, kvspec, kvspec, _const_spec(tt.shape)],
        out_specs=qspec,
        out_shape=jax.ShapeDtypeStruct((b, s, width), F32),
        scratch_shapes=[pltpu.VMEM((2 * pairs_per_step, s, LANES), F32),
                        pltpu.VMEM((nb, SUBLANES, LANES), F32)],
        compiler_params=pltpu.CompilerParams(
            dimension_semantics=("arbitrary", "arbitrary"), vmem_limit_bytes=VMEM_LIMIT_BYTES),
        name="sb_attention",
    )(q, kc, vc, tt)


def _mlstm_kernel(q_ref, k_ref, v_ref, vt_ref, g_ref, gb_ref, cs_ref, h_ref, st_ref, m_ref,
                  *, ts, chunk, d, nh):
    @pl.when(pl.program_id(1) == 0)
    def _():
        st_ref[...] = jnp.zeros_like(st_ref)
        m_ref[...] = jnp.zeros_like(m_ref)

    r_i = lax.broadcasted_iota(jnp.int32, (chunk, chunk), 0)
    c_i = lax.broadcasted_iota(jnp.int32, (chunk, chunk), 1)
    lower = c_i <= r_i
    ones_v = jnp.ones((chunk, d), BF16)

    states = [st_ref[h] for h in range(nh)]
    m_run = [m_ref[h, 0:1, :] for h in range(nh)]
    for c in range(ts // chunk):
        sl = slice(c * chunk, (c + 1) * chunk)
        gates = g_ref[:, sl] + gb_ref[...]
        lf_all = _log_sigmoid(gates)
        hi, lo = _split_bf16(lf_all)
        bcum = _dot(jnp.concatenate([hi, lo], axis=1), cs_ref[...])
        for h in range(nh):
            hs = slice(h * d, (h + 1) * d)
            q = q_ref[sl, hs]
            k = k_ref[sl, hs]
            v = v_ref[sl, hs]
            vt = vt_ref[hs, sl]
            ig = gates[h:h + 1, :]
            lf = lf_all[nh + h:nh + h + 1, :]
            b = bcum[nh + h:nh + h + 1, :]
            u = ig - b
            b_last = b[:, chunk - 1:chunk]
            m_loc = jnp.max(b_last + u, axis=-1, keepdims=True)
            m_prev = m_run[h]
            state = states[h]

            b_col = jnp.sum(jnp.where(lower, lf, 0.0), axis=-1, keepdims=True)
            w = jnp.where(lower, u, -jnp.inf)
            m_row = jnp.maximum(jnp.max(w, axis=-1, keepdims=True), m_prev[:, 0:1])
            scores = (_dot_nt(q, k) * jnp.exp(w - m_row)).astype(BF16)
            intra = _dot(scores, jnp.concatenate([v, ones_v], axis=1))
            inter = _dot_nt(q, state.astype(BF16))
            carry_scale = jnp.exp(m_prev[:, 0:1] - m_row)
            num = intra[:, :d] + carry_scale * inter[:, :d]
            den = intra[:, d:] + carry_scale * inter[:, d:]
            h_ref[sl, hs] = num / jnp.maximum(jnp.abs(den), jnp.exp(-(b_col + m_row)))

            w_loc = jnp.exp(b_last + u - m_loc)
            lhs = jnp.concatenate([vt.astype(F32) * w_loc, jnp.broadcast_to(w_loc, (d, chunk))], axis=0)
            upd = _dot(lhs.astype(BF16), k)
            m_new = jnp.maximum(b_last + m_prev, m_loc)
            states[h] = jnp.exp(b_last + m_prev - m_new) * state + jnp.exp(m_loc - m_new) * upd
            m_run[h] = m_new
    for h in range(nh):
        st_ref[h] = states[h]
        m_ref[h, 0:1, :] = m_run[h]


def _mlstm(q, k, v, vt, gates, gate_bias, *, ts, chunk=LANES):
    b, s, width = q.shape
    nh = ML_HEADS
    d = width // nh
    assert d == LANES and chunk == LANES
    j = lax.broadcasted_iota(jnp.int32, (chunk, chunk), 0)
    t = lax.broadcasted_iota(jnp.int32, (chunk, chunk), 1)
    triu = (j <= t).astype(BF16)
    cs = jnp.concatenate([triu, triu], axis=0)
    row = pl.BlockSpec((None, ts, width), lambda bi, i: (bi, i, 0))
    return pl.pallas_call(
        functools.partial(_mlstm_kernel, ts=ts, chunk=chunk, d=d, nh=nh),
        grid=(b, s // ts),
        in_specs=[row, row, row,
                  pl.BlockSpec((None, width, ts), lambda bi, i: (bi, 0, i)),
                  pl.BlockSpec((None, 2 * nh, ts), lambda bi, i: (bi, 0, i)),
                  _const_spec((2 * nh, 1)),
                  _const_spec(cs.shape)],
        out_specs=row,
        out_shape=jax.ShapeDtypeStruct((b, s, width), F32),
        scratch_shapes=[pltpu.VMEM((nh, 2 * d, d), F32), pltpu.VMEM((nh, SUBLANES, LANES), F32)],
        compiler_params=pltpu.CompilerParams(
            dimension_semantics=("arbitrary", "arbitrary"), vmem_limit_bytes=VMEM_LIMIT_BYTES),
        name="mlstm",
    )(q, k, v, vt, gates, gate_bias, cs)


def _out_kernel(ysb_ref, gz_ref, hml_ref, go_ref, gmz_ref, x_ref, p_ref,
                sbw_ref, mlw_ref, gsb_ref, gml_ref, wout_ref, pw_ref, wgate_ref, bgate_ref, wup_ref,
                o_ref, *, half):
    ysb = ysb_ref[...]
    ms_sb = _dot((ysb * ysb).astype(BF16), gsb_ref[...])
    a = (ysb * lax.rsqrt(ms_sb + EPS)) * sbw_ref[...] * gz_ref[...].astype(F32)
    yml = go_ref[...].astype(F32) * hml_ref[...]
    ms_ml = _dot((yml * yml).astype(BF16), gml_ref[...])
    bb = (yml * lax.rsqrt(ms_ml + EPS)) * mlw_ref[...] * gmz_ref[...].astype(F32)
    y = _dot(a.astype(BF16), wout_ref[0:half, :]) + _dot(bb.astype(BF16), wout_ref[half:2 * half, :])
    ms_y = jnp.mean(y * y, axis=-1, keepdims=True)
    h = x_ref[...] + (y * lax.rsqrt(ms_y + EPS)) * pw_ref[...]
    gate = _sigmoid(_dot(h.astype(BF16), wgate_ref[...]) + bgate_ref[...])
    o_ref[...] = h + gate * _dot(p_ref[...].astype(BF16), wup_ref[...])


def _head_mean_matrix(width, heads):
    hd = width // heads
    j = lax.broadcasted_iota(jnp.int32, (width, width), 0) // hd
    t = lax.broadcasted_iota(jnp.int32, (width, width), 1) // hd
    return jnp.where(j == t, 1.0 / hd, 0.0).astype(BF16)


def _out_stage(ysb, gz, hml, go, gmz, x, p, sb_norm_w, ml_norm_w, w_out, post_norm_w,
               w_gate, b_gate, w_up, *, ts):
    t, d = x.shape
    half = d // 2
    pd = p.shape[-1]
    rows = lambda w: pl.BlockSpec((ts, w), lambda i: (i, 0))
    return pl.pallas_call(
        functools.partial(_out_kernel, half=half),
        grid=(t // ts,),
        in_specs=[rows(half), rows(half), rows(half), rows(half), rows(half), rows(d), rows(pd),
                  _const_spec((1, half)), _const_spec((1, half)),
                  _const_spec((half, half)), _const_spec((half, half)),
                  _const_spec((d, d)), _const_spec((1, d)),
                  _const_spec((d, d)), _const_spec((1, d)), _const_spec((pd, d))],
        out_specs=rows(d),
        out_shape=jax.ShapeDtypeStruct((t, d), F32),
        compiler_params=pltpu.CompilerParams(
            dimension_semantics=("arbitrary",), vmem_limit_bytes=VMEM_LIMIT_BYTES),
        name="out_stage",
    )(ysb, gz, hml, go, gmz, x, p,
      sb_norm_w.reshape(1, half), ml_norm_w.reshape(1, half),
      _head_mean_matrix(half, SB_HEADS), _head_mean_matrix(half, ML_HEADS),
      w_out, post_norm_w.reshape(1, d), w_gate, b_gate.reshape(1, d), w_up)


def _layer(h, p, pre_norm_w, w_in, ml_conv_w, ml_conv_b, ml_i_bias, ml_f_bias, sb_norm_w, ml_norm_w,
           w_out, post_norm_w, ple_w_up, ple_w_gate, ple_b_gate):
    b, s, d = h.shape
    half = d // 2
    tiles = _tiles(s)
    wb = w_in.astype(BF16)
    n_main = 9 * half
    q, kc, vc, gz, mq, mk, mv, mvt, go, gmz, gates = _inproj(
        h, pre_norm_w, wb[:, :n_main], wb[:, 6 * half:7 * half].T, wb[:, n_main:].T,
        ml_conv_w, ml_conv_b, ts=tiles["inproj"])
    ysb = _sb_attention(q, kc, vc, pairs_per_step=tiles["sb_pairs"])
    gate_bias = jnp.concatenate([ml_i_bias, ml_f_bias]).reshape(2 * ML_HEADS, 1)
    hml = _mlstm(mq, mk, mv, mvt, gates, gate_bias, ts=tiles["ml"])
    flat = lambda a: a.reshape(b * s, a.shape[-1])
    out = _out_stage(flat(ysb), flat(gz), flat(hml), flat(go), flat(gmz), flat(h), flat(p),
                     sb_norm_w, ml_norm_w, w_out.astype(BF16), post_norm_w,
                     ple_w_gate.astype(BF16), ple_b_gate, ple_w_up.astype(BF16), ts=tiles["out"])
    return out.reshape(b, s, d)


def kernel(x, p, pre_norm_w, w_in, ml_conv_w, ml_conv_b, ml_i_bias, ml_f_bias, sb_norm_w, ml_norm_w,
           w_out, post_norm_w, ple_w_up, ple_w_gate, ple_b_gate):
    h = x
    for i in range(pre_norm_w.shape[0]):
        h = _layer(h, p[i], pre_norm_w[i], w_in[i], ml_conv_w[i], ml_conv_b[i], ml_i_bias[i], ml_f_bias[i],
                   sb_norm_w[i], ml_norm_w[i], w_out[i], post_norm_w[i], ple_w_up[i], ple_w_gate[i],
                   ple_b_gate[i])
    return h
```

```python
import functools

import jax
import jax.numpy as jnp
from jax import lax
from jax.experimental import pallas as pl
from jax.experimental.pallas import tpu as pltpu

SB_HEADS = 8
ML_HEADS = 4
CONV_K = 4
EPS = 1e-6
LOG2_E = 1.4426950408889634
SB_DEAD_CARRY = 152.0
SB_WINDOW = 3
SB_SWEEP_UNROLL = 2

LANES = 128
SUBLANES = 8
MXU_WIDTH = 256
VMEM_LIMIT_BYTES = 56 * 1024 * 1024

F32 = jnp.float32
BF16 = jnp.bfloat16


def _tiles(seq):
    def pick(pref):
        t = min(pref, seq)
        assert seq % t == 0 and t % LANES == 0
        return t
    return dict(inproj=pick(512), sb_pairs=2, ml=pick(1024), out=pick(1024))


def _dot(a, b):
    return jnp.dot(a, b, preferred_element_type=F32)


def _dot_nt(a, b):
    return lax.dot_general(a, b, (((1,), (1,)), ((), ())), preferred_element_type=F32)


def _split_bf16(x):
    hi = x.astype(BF16)
    lo = (x - hi.astype(F32)).astype(BF16)
    return hi, lo


def _sigmoid(x):
    return 0.5 + 0.5 * jnp.tanh(0.5 * x)


def _silu(x):
    h = 0.5 * x
    return h + h * jnp.tanh(h)


def _softplus(x):
    return jnp.maximum(x, 0.0) + jnp.log(1.0 + jnp.exp(-jnp.abs(x)))


def _log_sigmoid(x):
    return -_softplus(-x)


def _softplus2(x):
    return jnp.maximum(x, 0.0) + jnp.log2(1.0 + jnp.exp2(-jnp.abs(x)))


def _const_spec(shape):
    return pl.BlockSpec(shape, lambda *_: (0,) * len(shape), pipeline_mode=pl.Buffered(1))


def _store_pair_masked(dst_ref, a, ts, pair0):
    first_head = lax.broadcasted_iota(jnp.int32, (LANES, LANES), 1) < LANES // 2
    zero = jnp.zeros((LANES, LANES), a.dtype)
    for p in range(a.shape[1] // LANES):
        for j in range(ts // LANES):
            blk = a[j * LANES:(j + 1) * LANES, p * LANES:(p + 1) * LANES]
            dst_ref[pair0 + p, j, 0:LANES, :] = jnp.where(first_head, blk, zero)
            dst_ref[pair0 + p, j, LANES:2 * LANES, :] = jnp.where(first_head, zero, blk)


def _inproj_kernel(x_ref, pnw_ref, w_ref, wvt_ref, wg_ref, cw_ref, cb_ref,
                   q_ref, kc_ref, vc_ref, gz_ref, mq_ref, mk_ref, mv_ref, mvt_ref,
                   go_ref, gmz_ref, gates_ref, conv_buf, u_ref, *, ts, half, sb_scale, ml_scale):
    @pl.when(pl.program_id(1) == 0)
    def _():
        conv_buf[0:SUBLANES, :] = jnp.zeros((SUBLANES, 2 * half), F32)

    x = x_ref[...]
    ms = jnp.mean(x * x, axis=-1, keepdims=True)
    u_ref[...] = ((x * lax.rsqrt(ms + EPS)) * pnw_ref[...]).astype(BF16)

    cw = MXU_WIDTH
    per_seg = half // cw

    def proj(seg, sub):
        col = seg * half + sub * cw
        return _dot(u_ref[...], w_ref[:, col:col + cw])

    def conv_silu(seg, sub):
        cols = slice((seg - 4) * half + sub * cw, (seg - 4) * half + (sub + 1) * cw)
        conv_buf[SUBLANES:SUBLANES + ts, cols] = proj(seg, sub)
        full = conv_buf[:, cols]
        acc = cb_ref[:, cols]
        for j in range(CONV_K):
            delay = CONV_K - 1 - j
            tap = full if delay == 0 else pltpu.roll(full, delay, axis=0)
            acc = acc + cw_ref[j:j + 1, cols] * tap[SUBLANES:SUBLANES + ts]
        conv_buf[0:SUBLANES, cols] = full[ts:ts + SUBLANES]
        return _silu(acc)

    for sub in range(per_seg):
        out = slice(sub * cw, (sub + 1) * cw)
        mq_ref[:, out] = conv_silu(4, sub).astype(BF16)
        mk_ref[:, out] = (conv_silu(5, sub) * ml_scale).astype(BF16)
        q_ref[:, out] = (proj(0, sub) * sb_scale).astype(BF16)
        _store_pair_masked(kc_ref, proj(1, sub).astype(BF16), ts, sub * (cw // LANES))
        _store_pair_masked(vc_ref, proj(2, sub).astype(BF16), ts, sub * (cw // LANES))
        gz_ref[:, out] = _silu(proj(3, sub)).astype(BF16)
        mv_ref[:, out] = proj(6, sub).astype(BF16)
        go_ref[:, out] = _sigmoid(proj(7, sub)).astype(BF16)
        gmz_ref[:, out] = _silu(proj(8, sub)).astype(BF16)
    mvt_ref[...] = _dot_nt(wvt_ref[...], u_ref[...]).astype(BF16)
    gates_ref[...] = _dot_nt(wg_ref[...], u_ref[...])


def _inproj(x, pre_norm_w, w_main, w_vt, w_gates, conv_w, conv_b, *, ts):
    b, s, d = x.shape
    half = d // 2
    n_gates = w_gates.shape[0]
    sb_scale = float(half // SB_HEADS) ** -0.5 * LOG2_E
    ml_scale = float(half // ML_HEADS) ** -0.5
    row = pl.BlockSpec((None, ts, half), lambda bi, i: (bi, i, 0))
    act = jax.ShapeDtypeStruct((b, s, half), BF16)
    pairs = half // LANES
    pair_blocks = pl.BlockSpec((None, pairs, ts // LANES, 2 * LANES, LANES), lambda bi, i: (bi, 0, i, 0, 0))
    pair_act = jax.ShapeDtypeStruct((b, pairs, s // LANES, 2 * LANES, LANES), BF16)
    return pl.pallas_call(
        functools.partial(_inproj_kernel, ts=ts, half=half, sb_scale=sb_scale, ml_scale=ml_scale),
        grid=(b, s // ts),
        in_specs=[
            pl.BlockSpec((None, ts, d), lambda bi, i: (bi, i, 0)),
            _const_spec((1, d)),
            _const_spec(w_main.shape),
            _const_spec(w_vt.shape),
            _const_spec(w_gates.shape),
            _const_spec(conv_w.shape),
            _const_spec((1, 2 * half)),
        ],
        out_specs=[row, pair_blocks, pair_blocks, row, row, row, row,
                   pl.BlockSpec((None, half, ts), lambda bi, i: (bi, 0, i)),
                   row, row,
                   pl.BlockSpec((None, n_gates, ts), lambda bi, i: (bi, 0, i))],
        out_shape=[act, pair_act, pair_act, act, act, act, act,
                   jax.ShapeDtypeStruct((b, half, s), BF16),
                   act, act,
                   jax.ShapeDtypeStruct((b, n_gates, s), F32)],
        scratch_shapes=[pltpu.VMEM((ts + SUBLANES, 2 * half), F32), pltpu.VMEM((ts, d), BF16)],
        compiler_params=pltpu.CompilerParams(
            dimension_semantics=("arbitrary", "arbitrary"), vmem_limit_bytes=VMEM_LIMIT_BYTES),
        name="inproj",
    )(x, pre_norm_w.reshape(1, d), w_main, w_vt, w_gates, conv_w, conv_b.reshape(1, 2 * half))


def _sb_kernel(q_ref, kc_ref, vc_ref, tt_ref, y_ref, carry_ref, exit_min_ref, *, seq, pairs, unroll):
    nb = seq // LANES
    y_ref[...] = jnp.zeros_like(y_ref)
    carry_ref[...] = jnp.zeros_like(carry_ref)
    visible = (lax.broadcasted_iota(jnp.int32, (SB_WINDOW * LANES, LANES), 1)
               < lax.broadcasted_iota(jnp.int32, (SB_WINDOW * LANES, LANES), 0))

    def block(kb, sub0, n_sub, diagonal):
        rows = pl.ds(pl.multiple_of(sub0 * LANES, LANES), n_sub * LANES)

        def mask_diag(x):
            return jnp.where(visible[:n_sub * LANES], x, 0.0) if diagonal else x

        for p in range(pairs):
            lanes = slice(p * LANES, (p + 1) * LANES)
            z = _dot_nt(q_ref[rows, lanes], kc_ref[p, kb])
            a_parts = []
            for hh in range(2):
                zh = z[:, hh * LANES:(hh + 1) * LANES]
                hi, lo = _split_bf16(mask_diag(_softplus2(zh)))
                ct = _dot(jnp.concatenate([hi, lo], axis=1), tt_ref[...])
                carry = carry_ref[2 * p + hh, rows, :]
                a = mask_diag(jnp.exp2(zh - (ct[:, :LANES] + carry)))
                carry_ref[2 * p + hh, rows, :] = carry + ct[:, LANES:]
                a_parts.append(a.astype(BF16))
            y_ref[rows, lanes] += _dot(jnp.concatenate(a_parts, axis=1), vc_ref[p, kb])

    def min_carry(sub):
        rows = pl.ds(pl.multiple_of(sub * LANES, LANES), LANES)
        m = carry_ref[0, rows, :]
        for hh in range(1, 2 * pairs):
            m = jnp.minimum(m, carry_ref[hh, rows, :])
        return jnp.min(m, axis=0, keepdims=True)

    for back in range(1, SB_WINDOW):
        block(nb - back, nb - back, back, True)

    def sweep(j, carry):
        for step in range(unroll):
            kb = nb - SB_WINDOW - (j * unroll + step)
            block(kb, kb, SB_WINDOW, True)
            leaving = kb + SB_WINDOW - 1
            exit_min_ref[leaving] = jnp.broadcast_to(min_carry(leaving), (SUBLANES, LANES))
        return carry

    lax.fori_loop(0, (nb - SB_WINDOW + 1) // unroll, sweep, 0)

    def finish(sub, carry):
        def body(state):
            kb, _ = state
            block(kb, sub, 1, False)
            return kb - 1, jnp.min(min_carry(sub))

        def live(state):
            kb, m = state
            return jnp.logical_and(kb >= 0, m < SB_DEAD_CARRY)

        lax.while_loop(live, body, (sub - SB_WINDOW, jnp.min(exit_min_ref[sub])))
        return carry

    @pl.when(jnp.min(exit_min_ref[SB_WINDOW:nb]) < SB_DEAD_CARRY)
    def _():
        lax.fori_loop(SB_WINDOW, nb, finish, 0)


def _sb_attention(q, kc, vc, *, pairs_per_step):
    b, s, width = q.shape
    pairs = width // LANES
    nb = s // LANES
    assert pairs % pairs_per_step == 0 and nb > SB_WINDOW
    sweep_blocks = nb - SB_WINDOW + 1
    unroll = max(u for u in range(1, SB_SWEEP_UNROLL + 1) if sweep_blocks % u == 0)
    pw = pairs_per_step * LANES
    j = lax.broadcasted_iota(jnp.int32, (LANES, LANES), 0)
    t = lax.broadcasted_iota(jnp.int32, (LANES, LANES), 1)
    tri_ones = jnp.concatenate([(j >= t).astype(BF16), jnp.ones((LANES, LANES), BF16)], axis=1)
    tt = jnp.concatenate([tri_ones, tri_ones], axis=0)
    qspec = pl.BlockSpec((None, s, pw), lambda bi, p: (bi, 0, p))
    kvspec = pl.BlockSpec((None, pairs_per_step, nb, 2 * LANES, LANES), lambda bi, p: (bi, p, 0, 0, 0))
    return pl.pallas_call(
        functools.partial(_sb_kernel, seq=s, pairs=pairs_per_step, unroll=unroll),
        grid=(b, pairs // pairs_per_step),
        in_specs=[qspec, kvspec, kvspec, _const_spec(tt.shape)],
        out_specs=qspec,
        out_shape=jax.ShapeDtypeStruct((b, s, width), F32),
        scratch_shapes=[pltpu.VMEM((2 * pairs_per_step, s, LANES), F32),
                        pltpu.VMEM((nb, SUBLANES, LANES), F32)],
        compiler_params=pltpu.CompilerParams(
            dimension_semantics=("arbitrary", "arbitrary"), vmem_limit_bytes=VMEM_LIMIT_BYTES),
        name="sb_attention",
    )(q, kc, vc, tt)


def _mlstm_kernel(q_ref, k_ref, v_ref, vt_ref, g_ref, gb_ref, cs_ref, h_ref, st_ref, m_ref,
                  *, ts, chunk, d, nh):
    @pl.when(pl.program_id(1) == 0)
    def _():
        st_ref[...] = jnp.zeros_like(st_ref)
        m_ref[...] = jnp.zeros_like(m_ref)

    r_i = lax.broadcasted_iota(jnp.int32, (chunk, chunk), 0)
    c_i = lax.broadcasted_iota(jnp.int32, (chunk, chunk), 1)
    lower = c_i <= r_i
    ones_v = jnp.ones((chunk, d), BF16)

    states = [st_ref[h] for h in range(nh)]
    m_run = [m_ref[h, 0:1, :] for h in range(nh)]
    for c in range(ts // chunk):
        sl = slice(c * chunk, (c + 1) * chunk)
        gates = g_ref[:, sl] + gb_ref[...]
        lf_all = _log_sigmoid(gates)
        hi, lo = _split_bf16(lf_all)
        bcum = _dot(jnp.concatenate([hi, lo], axis=1), cs_ref[...])
        for h in range(nh):
            hs = slice(h * d, (h + 1) * d)
            q = q_ref[sl, hs]
            k = k_ref[sl, hs]
            v = v_ref[sl, hs]
            vt = vt_ref[hs, sl]
            ig = gates[h:h + 1, :]
            lf = lf_all[nh + h:nh + h + 1, :]
            b = bcum[nh + h:nh + h + 1, :]
            u = ig - b
            b_last = b[:, chunk - 1:chunk]
            m_loc = jnp.max(b_last + u, axis=-1, keepdims=True)
            m_prev = m_run[h]
            state = states[h]

            b_col = jnp.sum(jnp.where(lower, lf, 0.0), axis=-1, keepdims=True)
            w = jnp.where(lower, u, -jnp.inf)
            m_row = jnp.maximum(jnp.max(w, axis=-1, keepdims=True), m_prev[:, 0:1])
            scores = (_dot_nt(q, k) * jnp.exp(w - m_row)).astype(BF16)
            intra = _dot(scores, jnp.concatenate([v, ones_v], axis=1))
            inter = _dot_nt(q, state.astype(BF16))
            carry_scale = jnp.exp(m_prev[:, 0:1] - m_row)
            num = intra[:, :d] + carry_scale * inter[:, :d]
            den = intra[:, d:] + carry_scale * inter[:, d:]
            h_ref[sl, hs] = num / jnp.maximum(jnp.abs(den), jnp.exp(-(b_col + m_row)))

            w_loc = jnp.exp(b_last + u - m_loc)
            lhs = jnp.concatenate([vt.astype(F32) * w_loc, jnp.broadcast_to(w_loc, (d, chunk))], axis=0)
            upd = _dot(lhs.astype(BF16), k)
            m_new = jnp.maximum(b_last + m_prev, m_loc)
            states[h] = jnp.exp(b_last + m_prev - m_new) * state + jnp.exp(m_loc - m_new) * upd
            m_run[h] = m_new
    for h in range(nh):
        st_ref[h] = states[h]
        m_ref[h, 0:1, :] = m_run[h]


def _mlstm(q, k, v, vt, gates, gate_bias, *, ts, chunk=LANES):
    b, s, width = q.shape
    nh = ML_HEADS
    d = width // nh
    assert d == LANES and chunk == LANES
    j = lax.broadcasted_iota(jnp.int32, (chunk, chunk), 0)
    t = lax.broadcasted_iota(jnp.int32, (chunk, chunk), 1)
    triu = (j <= t).astype(BF16)
    cs = jnp.concatenate([triu, triu], axis=0)
    row = pl.BlockSpec((None, ts, width), lambda bi, i: (bi, i, 0))
    return pl.pallas_call(
        functools.partial(_mlstm_kernel, ts=ts, chunk=chunk, d=d, nh=nh),
        grid=(b, s // ts),
        in_specs=[row, row, row,
                  pl.BlockSpec((None, width, ts), lambda bi, i: (bi, 0, i)),
                  pl.BlockSpec((None, 2 * nh, ts), lambda bi, i: (bi, 0, i)),
                  _const_spec((2 * nh, 1)),
                  _const_spec(cs.shape)],
        out_specs=row,
        out_shape=jax.ShapeDtypeStruct((b, s, width), F32),
        scratch_shapes=[pltpu.VMEM((nh, 2 * d, d), F32), pltpu.VMEM((nh, SUBLANES, LANES), F32)],
        compiler_params=pltpu.CompilerParams(
            dimension_semantics=("arbitrary", "arbitrary"), vmem_limit_bytes=VMEM_LIMIT_BYTES),
        name="mlstm",
    )(q, k, v, vt, gates, gate_bias, cs)


def _out_kernel(ysb_ref, gz_ref, hml_ref, go_ref, gmz_ref, x_ref, p_ref,
                sbw_ref, mlw_ref, gsb_ref, gml_ref, wout_ref, pw_ref, wgate_ref, bgate_ref, wup_ref,
                o_ref, *, half):
    def head_mean_square(y, g_ref):
        yy = (y * y).astype(BF16)
        return jnp.concatenate([_dot(yy[:, c:c + MXU_WIDTH], g_ref[...]) for c in range(0, half, MXU_WIDTH)], axis=1)

    ysb = ysb_ref[...]
    a = (ysb * lax.rsqrt(head_mean_square(ysb, gsb_ref) + EPS)) * sbw_ref[...] * gz_ref[...].astype(F32)
    yml = go_ref[...].astype(F32) * hml_ref[...]
    bb = (yml * lax.rsqrt(head_mean_square(yml, gml_ref) + EPS)) * mlw_ref[...] * gmz_ref[...].astype(F32)
    y = _dot(a.astype(BF16), wout_ref[0:half, :]) + _dot(bb.astype(BF16), wout_ref[half:2 * half, :])
    ms_y = jnp.mean(y * y, axis=-1, keepdims=True)
    h = x_ref[...] + (y * lax.rsqrt(ms_y + EPS)) * pw_ref[...]
    gate = _sigmoid(_dot(h.astype(BF16), wgate_ref[...]) + bgate_ref[...])
    o_ref[...] = h + gate * _dot(p_ref[...].astype(BF16), wup_ref[...])


def _head_mean_matrix(hd):
    assert MXU_WIDTH % hd == 0
    j = lax.broadcasted_iota(jnp.int32, (MXU_WIDTH, MXU_WIDTH), 0) // hd
    t = lax.broadcasted_iota(jnp.int32, (MXU_WIDTH, MXU_WIDTH), 1) // hd
    return jnp.where(j == t, 1.0 / hd, 0.0).astype(BF16)


def _out_stage(ysb, gz, hml, go, gmz, x, p, sb_norm_w, ml_norm_w, w_out, post_norm_w,
               w_gate, b_gate, w_up, *, ts):
    t, d = x.shape
    half = d // 2
    pd = p.shape[-1]
    rows = lambda w: pl.BlockSpec((ts, w), lambda i: (i, 0))
    return pl.pallas_call(
        functools.partial(_out_kernel, half=half),
        grid=(t // ts,),
        in_specs=[rows(half), rows(half), rows(half), rows(half), rows(half), rows(d), rows(pd),
                  _const_spec((1, half)), _const_spec((1, half)),
                  _const_spec((MXU_WIDTH, MXU_WIDTH)), _const_spec((MXU_WIDTH, MXU_WIDTH)),
                  _const_spec((d, d)), _const_spec((1, d)),
                  _const_spec((d, d)), _const_spec((1, d)), _const_spec((pd, d))],
        out_specs=rows(d),
        out_shape=jax.ShapeDtypeStruct((t, d), F32),
        compiler_params=pltpu.CompilerParams(
            dimension_semantics=("arbitrary",), vmem_limit_bytes=VMEM_LIMIT_BYTES),
        name="out_stage",
    )(ysb, gz, hml, go, gmz, x, p,
      sb_norm_w.reshape(1, half), ml_norm_w.reshape(1, half),
      _head_mean_matrix(half // SB_HEADS), _head_mean_matrix(half // ML_HEADS),
      w_out, post_norm_w.reshape(1, d), w_gate, b_gate.reshape(1, d), w_up)


def _layer(h, p, pre_norm_w, w_in, ml_conv_w, ml_conv_b, ml_i_bias, ml_f_bias, sb_norm_w, ml_norm_w,
           w_out, post_norm_w, ple_w_up, ple_w_gate, ple_b_gate):
    b, s, d = h.shape
    half = d // 2
    tiles = _tiles(s)
    wb = w_in.astype(BF16)
    n_main = 9 * half
    q, kc, vc, gz, mq, mk, mv, mvt, go, gmz, gates = _inproj(
        h, pre_norm_w, wb[:, :n_main], wb[:, 6 * half:7 * half].T, wb[:, n_main:].T,
        ml_conv_w, ml_conv_b, ts=tiles["inproj"])
    ysb = _sb_attention(q, kc, vc, pairs_per_step=tiles["sb_pairs"])
    gate_bias = jnp.concatenate([ml_i_bias, ml_f_bias]).reshape(2 * ML_HEADS, 1)
    hml = _mlstm(mq, mk, mv, mvt, gates, gate_bias, ts=tiles["ml"])
    flat = lambda a: a.reshape(b * s, a.shape[-1])
    out = _out_stage(flat(ysb), flat(gz), flat(hml), flat(go), flat(gmz), flat(h), flat(p),
                     sb_norm_w, ml_norm_w, w_out.astype(BF16), post_norm_w,
                     ple_w_gate.astype(BF16), ple_b_gate, ple_w_up.astype(BF16), ts=tiles["out"])
    return out.reshape(b, s, d)


def kernel(x, p, pre_norm_w, w_in, ml_conv_w, ml_conv_b, ml_i_bias, ml_f_bias, sb_norm_w, ml_norm_w,
           w_out, post_norm_w, ple_w_up, ple_w_gate, ple_b_gate):
    h = x
    for i in range(pre_norm_w.shape[0]):
        h = _layer(h, p[i], pre_norm_w[i], w_in[i], ml_conv_w[i], ml_conv_b[i], ml_i_bias[i], ml_f_bias[i],
                   sb_norm_w[i], ml_norm_w[i], w_out[i], post_norm_w[i], ple_w_up[i], ple_w_gate[i],
                   ple_b_gate[i])
    return h
```

```python
import functools

import jax
import jax.numpy as jnp
from jax import lax
from jax.experimental import pallas as pl
from jax.experimental.pallas import tpu as pltpu

SB_HEADS = 8
ML_HEADS = 4
CONV_K = 4
EPS = 1e-6
LOG2_E = 1.4426950408889634

LANES = 128
SUBLANES = 8
MXU_WIDTH = 256
VMEM_LIMIT_BYTES = 56 * 1024 * 1024

SB_DEAD_CARRY = 152.0
SB_THIN_ROWS = 32
SB_WINDOW_ROWS = 2 * LANES + SB_THIN_ROWS
SB_SWEEP_UNROLL = 10

F32 = jnp.float32
BF16 = jnp.bfloat16


def _tiles(seq):
    def pick(pref):
        t = min(pref, seq)
        assert seq % t == 0 and t % LANES == 0
        return t
    return dict(inproj=pick(512), sb_pairs=2, ml=pick(1024), out=pick(1024))


def _dot(a, b):
    return jnp.dot(a, b, preferred_element_type=F32)


def _dot_nt(a, b):
    return lax.dot_general(a, b, (((1,), (1,)), ((), ())), preferred_element_type=F32)


def _split_bf16(x):
    hi = x.astype(BF16)
    lo = (x - hi.astype(F32)).astype(BF16)
    return hi, lo


def _sigmoid(x):
    return 0.5 + 0.5 * jnp.tanh(0.5 * x)


def _silu(x):
    h = 0.5 * x
    return h + h * jnp.tanh(h)


def _softplus(x):
    return jnp.maximum(x, 0.0) + jnp.log(1.0 + jnp.exp(-jnp.abs(x)))


def _log_sigmoid(x):
    return -_softplus(-x)


def _softplus2(x):
    return jnp.maximum(x, 0.0) + jnp.log2(1.0 + jnp.exp2(-jnp.abs(x)))


def _const_spec(shape):
    return pl.BlockSpec(shape, lambda *_: (0,) * len(shape), pipeline_mode=pl.Buffered(1))


def _store_pair_masked(dst_ref, a, ts, pair0):
    first_head = lax.broadcasted_iota(jnp.int32, (LANES, LANES), 1) < LANES // 2
    zero = jnp.zeros((LANES, LANES), a.dtype)
    for p in range(a.shape[1] // LANES):
        for j in range(ts // LANES):
            blk = a[j * LANES:(j + 1) * LANES, p * LANES:(p + 1) * LANES]
            dst_ref[pair0 + p, j, 0:LANES, :] = jnp.where(first_head, blk, zero)
            dst_ref[pair0 + p, j, LANES:2 * LANES, :] = jnp.where(first_head, zero, blk)


def _inproj_kernel(x_ref, pnw_ref, w_ref, wvt_ref, wg_ref, cw_ref, cb_ref,
                   q_ref, kc_ref, vc_ref, gz_ref, mq_ref, mk_ref, mv_ref, mvt_ref,
                   go_ref, gmz_ref, gates_ref, conv_buf, u_ref, *, ts, half, sb_scale, ml_scale):
    @pl.when(pl.program_id(1) == 0)
    def _():
        conv_buf[0:SUBLANES, :] = jnp.zeros((SUBLANES, 2 * half), F32)

    x = x_ref[...]
    ms = jnp.mean(x * x, axis=-1, keepdims=True)
    u_ref[...] = ((x * lax.rsqrt(ms + EPS)) * pnw_ref[...]).astype(BF16)

    cw = MXU_WIDTH
    per_seg = half // cw

    def proj(seg, sub):
        col = seg * half + sub * cw
        return _dot(u_ref[...], w_ref[:, col:col + cw])

    def conv_silu(seg, sub):
        cols = slice((seg - 4) * half + sub * cw, (seg - 4) * half + (sub + 1) * cw)
        conv_buf[SUBLANES:SUBLANES + ts, cols] = proj(seg, sub)
        full = conv_buf[:, cols]
        acc = cb_ref[:, cols]
        for j in range(CONV_K):
            delay = CONV_K - 1 - j
            tap = full if delay == 0 else pltpu.roll(full, delay, axis=0)
            acc = acc + cw_ref[j:j + 1, cols] * tap[SUBLANES:SUBLANES + ts]
        conv_buf[0:SUBLANES, cols] = full[ts:ts + SUBLANES]
        return _silu(acc)

    for sub in range(per_seg):
        out = slice(sub * cw, (sub + 1) * cw)
        mq_ref[:, out] = conv_silu(4, sub).astype(BF16)
        mk_ref[:, out] = (conv_silu(5, sub) * ml_scale).astype(BF16)
        q_ref[:, out] = (proj(0, sub) * sb_scale).astype(BF16)
        _store_pair_masked(kc_ref, proj(1, sub).astype(BF16), ts, sub * (cw // LANES))
        _store_pair_masked(vc_ref, proj(2, sub).astype(BF16), ts, sub * (cw // LANES))
        gz_ref[:, out] = _silu(proj(3, sub)).astype(BF16)
        mv_ref[:, out] = proj(6, sub).astype(BF16)
        go_ref[:, out] = _sigmoid(proj(7, sub)).astype(BF16)
        gmz_ref[:, out] = _silu(proj(8, sub)).astype(BF16)
    mvt_ref[...] = _dot_nt(wvt_ref[...], u_ref[...]).astype(BF16)
    gates_ref[...] = _dot_nt(wg_ref[...], u_ref[...])


def _inproj(x, pre_norm_w, w_main, w_vt, w_gates, conv_w, conv_b, *, ts):
    b, s, d = x.shape
    half = d // 2
    n_gates = w_gates.shape[0]
    sb_scale = float(half // SB_HEADS) ** -0.5 * LOG2_E
    ml_scale = float(half // ML_HEADS) ** -0.5
    row = pl.BlockSpec((None, ts, half), lambda bi, i: (bi, i, 0))
    act = jax.ShapeDtypeStruct((b, s, half), BF16)
    pairs = half // LANES
    pair_blocks = pl.BlockSpec((None, pairs, ts // LANES, 2 * LANES, LANES), lambda bi, i: (bi, 0, i, 0, 0))
    pair_act = jax.ShapeDtypeStruct((b, pairs, s // LANES, 2 * LANES, LANES), BF16)
    return pl.pallas_call(
        functools.partial(_inproj_kernel, ts=ts, half=half, sb_scale=sb_scale, ml_scale=ml_scale),
        grid=(b, s // ts),
        in_specs=[
            pl.BlockSpec((None, ts, d), lambda bi, i: (bi, i, 0)),
            _const_spec((1, d)),
            _const_spec(w_main.shape),
            _const_spec(w_vt.shape),
            _const_spec(w_gates.shape),
            _const_spec(conv_w.shape),
            _const_spec((1, 2 * half)),
        ],
        out_specs=[row, pair_blocks, pair_blocks, row, row, row, row,
                   pl.BlockSpec((None, half, ts), lambda bi, i: (bi, 0, i)),
                   row, row,
                   pl.BlockSpec((None, n_gates, ts), lambda bi, i: (bi, 0, i))],
        out_shape=[act, pair_act, pair_act, act, act, act, act,
                   jax.ShapeDtypeStruct((b, half, s), BF16),
                   act, act,
                   jax.ShapeDtypeStruct((b, n_gates, s), F32)],
        scratch_shapes=[pltpu.VMEM((ts + SUBLANES, 2 * half), F32), pltpu.VMEM((ts, d), BF16)],
        compiler_params=pltpu.CompilerParams(
            dimension_semantics=("arbitrary", "arbitrary"), vmem_limit_bytes=VMEM_LIMIT_BYTES),
        name="inproj",
    )(x, pre_norm_w.reshape(1, d), w_main, w_vt, w_gates, conv_w, conv_b.reshape(1, 2 * half))


def _sb_kernel(q_ref, kc_ref, vc_ref, tt_ref, y_ref, carry_ref, exit_min_ref, *, seq, pairs, unroll):
    nb = seq // LANES
    y_ref[...] = jnp.zeros_like(y_ref)
    carry_ref[...] = jnp.zeros_like(carry_ref)
    visible = (lax.broadcasted_iota(jnp.int32, (SB_WINDOW_ROWS, LANES), 1)
               < lax.broadcasted_iota(jnp.int32, (SB_WINDOW_ROWS, LANES), 0))

    def block(kb, row0, n_rows, diagonal):
        rows = pl.ds(pl.multiple_of(row0, SB_THIN_ROWS), n_rows)

        def mask_diag(x):
            return jnp.where(visible[:n_rows], x, 0.0) if diagonal else x

        for p in range(pairs):
            lanes = slice(p * LANES, (p + 1) * LANES)
            z = _dot_nt(q_ref[rows, lanes], kc_ref[p, kb])
            a_parts = []
            for hh in range(2):
                zh = z[:, hh * LANES:(hh + 1) * LANES]
                hi, lo = _split_bf16(mask_diag(_softplus2(zh)))
                ct = _dot(jnp.concatenate([hi, lo], axis=1), tt_ref[...])
                carry = carry_ref[2 * p + hh, rows, :]
                a = mask_diag(jnp.exp2(zh - (ct[:, :LANES] + carry)))
                carry_ref[2 * p + hh, rows, :] = carry + ct[:, LANES:]
                a_parts.append(a.astype(BF16))
            y_ref[rows, lanes] += _dot(jnp.concatenate(a_parts, axis=1), vc_ref[p, kb])

    def min_carry(sub):
        rows = pl.ds(pl.multiple_of(sub * LANES, LANES), LANES)
        m = carry_ref[0, rows, :]
        for hh in range(1, 2 * pairs):
            m = jnp.minimum(m, carry_ref[hh, rows, :])
        return jnp.min(m)

    block(nb - 1, (nb - 1) * LANES, LANES, True)
    block(nb - 2, (nb - 2) * LANES, 2 * LANES, True)

    def sweep(j, carry):
        for step in range(unroll):
            kb = nb - 3 - (j * unroll + step)
            block(kb, kb * LANES, SB_WINDOW_ROWS, True)
            exit_min_ref[kb + 2] = min_carry(kb + 2)
        return carry

    lax.fori_loop(0, (nb - 2) // unroll, sweep, 0)

    def finish(sub, carry):
        @pl.when(exit_min_ref[sub] < SB_DEAD_CARRY)
        def _():
            block(sub - 2, sub * LANES + SB_THIN_ROWS, LANES - SB_THIN_ROWS, False)

            def body(state):
                kb, _ = state
                block(kb, sub * LANES, LANES, False)
                return kb - 1, min_carry(sub)

            def live(state):
                kb, m = state
                return jnp.logical_and(kb >= 0, m < SB_DEAD_CARRY)

            lax.while_loop(live, body, (sub - 3, min_carry(sub)))
        return carry

    lax.fori_loop(2, nb, finish, 0)


def _sb_attention(q, kc, vc, *, pairs_per_step):
    b, s, width = q.shape
    pairs = width // LANES
    nb = s // LANES
    assert pairs % pairs_per_step == 0 and nb >= 4
    unroll = max(u for u in range(1, SB_SWEEP_UNROLL + 1) if (nb - 2) % u == 0)
    pw = pairs_per_step * LANES
    j = lax.broadcasted_iota(jnp.int32, (LANES, LANES), 0)
    t = lax.broadcasted_iota(jnp.int32, (LANES, LANES), 1)
    tri_ones = jnp.concatenate([(j >= t).astype(BF16), jnp.ones((LANES, LANES), BF16)], axis=1)
    tt = jnp.concatenate([tri_ones, tri_ones], axis=0)
    qspec = pl.BlockSpec((None, s, pw), lambda bi, p: (bi, 0, p))
    kvspec = pl.BlockSpec((None, pairs_per_step, nb, 2 * LANES, LANES), lambda bi, p: (bi, p, 0, 0, 0))
    return pl.pallas_call(
        functools.partial(_sb_kernel, seq=s, pairs=pairs_per_step, unroll=unroll),
        grid=(b, pairs // pairs_per_step),
        in_specs=[qspec, kvspec, kvspec, _const_spec(tt.shape)],
        out_specs=qspec,
        out_shape=jax.ShapeDtypeStruct((b, s, width), F32),
        scratch_shapes=[pltpu.VMEM((2 * pairs_per_step, s, LANES), F32),
                        pltpu.SMEM((nb,), F32)],
        compiler_params=pltpu.CompilerParams(
            dimension_semantics=("arbitrary", "arbitrary"), vmem_limit_bytes=VMEM_LIMIT_BYTES),
        name="sb_attention",
    )(q, kc, vc, tt)


def _mlstm_kernel(q_ref, k_ref, v_ref, vt_ref, g_ref, gb_ref, cs_ref, h_ref, st_ref, m_ref,
                  *, ts, chunk, d, nh):
    @pl.when(pl.program_id(1) == 0)
    def _():
        st_ref[...] = jnp.zeros_like(st_ref)
        m_ref[...] = jnp.zeros_like(m_ref)

    r_i = lax.broadcasted_iota(jnp.int32, (chunk, chunk), 0)
    c_i = lax.broadcasted_iota(jnp.int32, (chunk, chunk), 1)
    lower = c_i <= r_i
    ones_v = jnp.ones((chunk, d), BF16)

    states = [st_ref[h] for h in range(nh)]
    m_run = [m_ref[h, 0:1, :] for h in range(nh)]
    for c in range(ts // chunk):
        sl = slice(c * chunk, (c + 1) * chunk)
        gates = g_ref[:, sl] + gb_ref[...]
        lf_all = _log_sigmoid(gates)
        hi, lo = _split_bf16(lf_all)
        bcum = _dot(jnp.concatenate([hi, lo], axis=1), cs_ref[...])
        for h in range(nh):
            hs = slice(h * d, (h + 1) * d)
            q = q_ref[sl, hs]
            k = k_ref[sl, hs]
            v = v_ref[sl, hs]
            vt = vt_ref[hs, sl]
            ig = gates[h:h + 1, :]
            lf = lf_all[nh + h:nh + h + 1, :]
            b = bcum[nh + h:nh + h + 1, :]
            u = ig - b
            b_last = b[:, chunk - 1:chunk]
            m_loc = jnp.max(b_last + u, axis=-1, keepdims=True)
            m_prev = m_run[h]
            state = states[h]

            b_col = jnp.sum(jnp.where(lower, lf, 0.0), axis=-1, keepdims=True)
            w = jnp.where(lower, u, -jnp.inf)
            m_row = jnp.maximum(jnp.max(w, axis=-1, keepdims=True), m_prev[:, 0:1])
            scores = (_dot_nt(q, k) * jnp.exp(w - m_row)).astype(BF16)
            intra = _dot(scores, jnp.concatenate([v, ones_v], axis=1))
            inter = _dot_nt(q, state.astype(BF16))
            carry_scale = jnp.exp(m_prev[:, 0:1] - m_row)
            num = intra[:, :d] + carry_scale * inter[:, :d]
            den = intra[:, d:] + carry_scale * inter[:, d:]
            h_ref[sl, hs] = num / jnp.maximum(jnp.abs(den), jnp.exp(-(b_col + m_row)))

            w_loc = jnp.exp(b_last + u - m_loc)
            lhs = jnp.concatenate([vt.astype(F32) * w_loc, jnp.broadcast_to(w_loc, (d, chunk))], axis=0)
            upd = _dot(lhs.astype(BF16), k)
            m_new = jnp.maximum(b_last + m_prev, m_loc)
            states[h] = jnp.exp(b_last + m_prev - m_new) * state + jnp.exp(m_loc - m_new) * upd
            m_run[h] = m_new
    for h in range(nh):
        st_ref[h] = states[h]
        m_ref[h, 0:1, :] = m_run[h]


def _mlstm(q, k, v, vt, gates, gate_bias, *, ts, chunk=LANES):
    b, s, width = q.shape
    nh = ML_HEADS
    d = width // nh
    assert d == LANES and chunk == LANES
    j = lax.broadcasted_iota(jnp.int32, (chunk, chunk), 0)
    t = lax.broadcasted_iota(jnp.int32, (chunk, chunk), 1)
    triu = (j <= t).astype(BF16)
    cs = jnp.concatenate([triu, triu], axis=0)
    row = pl.BlockSpec((None, ts, width), lambda bi, i: (bi, i, 0))
    return pl.pallas_call(
        functools.partial(_mlstm_kernel, ts=ts, chunk=chunk, d=d, nh=nh),
        grid=(b, s // ts),
        in_specs=[row, row, row,
                  pl.BlockSpec((None, width, ts), lambda bi, i: (bi, 0, i)),
                  pl.BlockSpec((None, 2 * nh, ts), lambda bi, i: (bi, 0, i)),
                  _const_spec((2 * nh, 1)),
                  _const_spec(cs.shape)],
        out_specs=row,
        out_shape=jax.ShapeDtypeStruct((b, s, width), F32),
        scratch_shapes=[pltpu.VMEM((nh, 2 * d, d), F32), pltpu.VMEM((nh, SUBLANES, LANES), F32)],
        compiler_params=pltpu.CompilerParams(
            dimension_semantics=("arbitrary", "arbitrary"), vmem_limit_bytes=VMEM_LIMIT_BYTES),
        name="mlstm",
    )(q, k, v, vt, gates, gate_bias, cs)


def _out_kernel(ysb_ref, gz_ref, hml_ref, go_ref, gmz_ref, x_ref, p_ref,
                sbw_ref, mlw_ref, gsb_ref, gml_ref, wout_ref, pw_ref, wgate_ref, bgate_ref, wup_ref,
                o_ref, *, half):
    def head_mean_square(y, g_ref):
        yy = (y * y).astype(BF16)
        return jnp.concatenate([_dot(yy[:, c:c + MXU_WIDTH], g_ref[...]) for c in range(0, half, MXU_WIDTH)], axis=1)

    ysb = ysb_ref[...]
    a = (ysb * lax.rsqrt(head_mean_square(ysb, gsb_ref) + EPS)) * sbw_ref[...] * gz_ref[...].astype(F32)
    yml = go_ref[...].astype(F32) * hml_ref[...]
    bb = (yml * lax.rsqrt(head_mean_square(yml, gml_ref) + EPS)) * mlw_ref[...] * gmz_ref[...].astype(F32)
    y = _dot(a.astype(BF16), wout_ref[0:half, :]) + _dot(bb.astype(BF16), wout_ref[half:2 * half, :])
    ms_y = jnp.mean(y * y, axis=-1, keepdims=True)
    h = x_ref[...] + (y * lax.rsqrt(ms_y + EPS)) * pw_ref[...]
    gate = _sigmoid(_dot(h.astype(BF16), wgate_ref[...]) + bgate_ref[...])
    o_ref[...] = h + gate * _dot(p_ref[...].astype(BF16), wup_ref[...])


def _head_mean_matrix(hd):
    assert MXU_WIDTH % hd == 0
    j = lax.broadcasted_iota(jnp.int32, (MXU_WIDTH, MXU_WIDTH), 0) // hd
    t = lax.broadcasted_iota(jnp.int32, (MXU_WIDTH, MXU_WIDTH), 1) // hd
    return jnp.where(j == t, 1.0 / hd, 0.0).astype(BF16)


def _out_stage(ysb, gz, hml, go, gmz, x, p, sb_norm_w, ml_norm_w, w_out, post_norm_w,
               w_gate, b_gate, w_up, *, ts):
    t, d = x.shape
    half = d // 2
    pd = p.shape[-1]
    rows = lambda w: pl.BlockSpec((ts, w), lambda i: (i, 0))
    return pl.pallas_call(
        functools.partial(_out_kernel, half=half),
        grid=(t // ts,),
        in_specs=[rows(half), rows(half), rows(half), rows(half), rows(half), rows(d), rows(pd),
                  _const_spec((1, half)), _const_spec((1, half)),
                  _const_spec((MXU_WIDTH, MXU_WIDTH)), _const_spec((MXU_WIDTH, MXU_WIDTH)),
                  _const_spec((d, d)), _const_spec((1, d)),
                  _const_spec((d, d)), _const_spec((1, d)), _const_spec((pd, d))],
        out_specs=rows(d),
        out_shape=jax.ShapeDtypeStruct((t, d), F32),
        compiler_params=pltpu.CompilerParams(
            dimension_semantics=("arbitrary",), vmem_limit_bytes=VMEM_LIMIT_BYTES),
        name="out_stage",
    )(ysb, gz, hml, go, gmz, x, p,
      sb_norm_w.reshape(1, half), ml_norm_w.reshape(1, half),
      _head_mean_matrix(half // SB_HEADS), _head_mean_matrix(half // ML_HEADS),
      w_out, post_norm_w.reshape(1, d), w_gate, b_gate.reshape(1, d), w_up)


def _layer(h, p, pre_norm_w, w_in, ml_conv_w, ml_conv_b, ml_i_bias, ml_f_bias, sb_norm_w, ml_norm_w,
           w_out, post_norm_w, ple_w_up, ple_w_gate, ple_b_gate):
    b, s, d = h.shape
    half = d // 2
    tiles = _tiles(s)
    wb = w_in.astype(BF16)
    n_main = 9 * half
    q, kc, vc, gz, mq, mk, mv, mvt, go, gmz, gates = _inproj(
        h, pre_norm_w, wb[:, :n_main], wb[:, 6 * half:7 * half].T, wb[:, n_main:].T,
        ml_conv_w, ml_conv_b, ts=tiles["inproj"])
    ysb = _sb_attention(q, kc, vc, pairs_per_step=tiles["sb_pairs"])
    gate_bias = jnp.concatenate([ml_i_bias, ml_f_bias]).reshape(2 * ML_HEADS, 1)
    hml = _mlstm(mq, mk, mv, mvt, gates, gate_bias, ts=tiles["ml"])
    flat = lambda a: a.reshape(b * s, a.shape[-1])
    out = _out_stage(flat(ysb), flat(gz), flat(hml), flat(go), flat(gmz), flat(h), flat(p),
                     sb_norm_w, ml_norm_w, w_out.astype(BF16), post_norm_w,
                     ple_w_gate.astype(BF16), ple_b_gate, ple_w_up.astype(BF16), ts=tiles["out"])
    return out.reshape(b, s, d)


def kernel(x, p, pre_norm_w, w_in, ml_conv_w, ml_conv_b, ml_i_bias, ml_f_bias, sb_norm_w, ml_norm_w,
           w_out, post_norm_w, ple_w_up, ple_w_gate, ple_b_gate):
    h = x
    for i in range(pre_norm_w.shape[0]):
        h = _layer(h, p[i], pre_norm_w[i], w_in[i], ml_conv_w[i], ml_conv_b[i], ml_i_bias[i], ml_f_bias[i],
                   sb_norm_w[i], ml_norm_w[i], w_out[i], post_norm_w[i], ple_w_up[i], ple_w_gate[i],
                   ple_b_gate[i])
    return h
```

```python
import functools

import jax
import jax.numpy as jnp
from jax import lax
from jax.experimental import pallas as pl
from jax.experimental.pallas import tpu as pltpu

SB_HEADS = 8
ML_HEADS = 4
CONV_K = 4
EPS = 1e-6
LOG2_E = 1.4426950408889634

LANES = 128
SUBLANES = 8
MXU_WIDTH = 256
VMEM_LIMIT_BYTES = 56 * 1024 * 1024

SB_DEAD_CARRY = 152.0
SB_THIN_ROWS = 32
SB_WINDOW_ROWS = 2 * LANES + SB_THIN_ROWS
SB_SWEEP_UNROLL = 10

F32 = jnp.float32
BF16 = jnp.bfloat16


def _tiles(seq):
    def pick(pref):
        t = min(pref, seq)
        assert seq % t == 0 and t % LANES == 0
        return t
    return dict(inproj=pick(1024), sb_pairs=2, ml=pick(1024), out=pick(1024))


def _dot(a, b):
    return jnp.dot(a, b, preferred_element_type=F32)


def _dot_nt(a, b):
    return lax.dot_general(a, b, (((1,), (1,)), ((), ())), preferred_element_type=F32)


def _split_bf16(x):
    hi = x.astype(BF16)
    lo = (x - hi.astype(F32)).astype(BF16)
    return hi, lo


def _sigmoid(x):
    return 0.5 + 0.5 * jnp.tanh(0.5 * x)


def _silu(x):
    h = 0.5 * x
    return h + h * jnp.tanh(h)


def _softplus(x):
    return jnp.maximum(x, 0.0) + jnp.log(1.0 + jnp.exp(-jnp.abs(x)))


def _log_sigmoid(x):
    return -_softplus(-x)


def _softplus2(x):
    return jnp.maximum(x, 0.0) + jnp.log2(1.0 + jnp.exp2(-jnp.abs(x)))


def _const_spec(shape):
    return pl.BlockSpec(shape, lambda *_: (0,) * len(shape), pipeline_mode=pl.Buffered(1))


def _store_pair_masked(dst_ref, a, ts, pair0):
    first_head = lax.broadcasted_iota(jnp.int32, (LANES, LANES), 1) < LANES // 2
    zero = jnp.zeros((LANES, LANES), a.dtype)
    for p in range(a.shape[1] // LANES):
        for j in range(ts // LANES):
            blk = a[j * LANES:(j + 1) * LANES, p * LANES:(p + 1) * LANES]
            dst_ref[pair0 + p, j, 0:LANES, :] = jnp.where(first_head, blk, zero)
            dst_ref[pair0 + p, j, LANES:2 * LANES, :] = jnp.where(first_head, zero, blk)


def _inproj_kernel(x_ref, pnw_ref, w_ref, wg_ref, cw_ref, cb_ref,
                   q_ref, kc_ref, vc_ref, gz_ref, mq_ref, mk_ref, mv_ref, mvt_ref,
                   go_ref, gmz_ref, gates_ref, conv_buf, u_ref, *, ts, half, sb_scale, ml_scale):
    @pl.when(pl.program_id(1) == 0)
    def _():
        conv_buf[0:SUBLANES, :] = jnp.zeros((SUBLANES, 2 * half), F32)

    x = x_ref[...]
    ms = jnp.mean(x * x, axis=-1, keepdims=True)
    u_ref[...] = ((x * lax.rsqrt(ms + EPS)) * pnw_ref[...]).astype(BF16)

    cw = MXU_WIDTH
    per_seg = half // cw

    def proj(seg, sub):
        col = seg * half + sub * cw
        return _dot(u_ref[...], w_ref[:, col:col + cw])

    def conv_silu(seg, sub):
        cols = slice((seg - 4) * half + sub * cw, (seg - 4) * half + (sub + 1) * cw)
        conv_buf[SUBLANES:SUBLANES + ts, cols] = proj(seg, sub)
        full = conv_buf[:, cols]
        acc = cb_ref[:, cols]
        for j in range(CONV_K):
            delay = CONV_K - 1 - j
            tap = full if delay == 0 else pltpu.roll(full, delay, axis=0)
            acc = acc + cw_ref[j:j + 1, cols] * tap[SUBLANES:SUBLANES + ts]
        conv_buf[0:SUBLANES, cols] = full[ts:ts + SUBLANES]
        return _silu(acc)

    for sub in range(per_seg):
        out = slice(sub * cw, (sub + 1) * cw)
        mq_ref[:, out] = conv_silu(4, sub).astype(BF16)
        mk_ref[:, out] = (conv_silu(5, sub) * ml_scale).astype(BF16)
        q_ref[:, out] = (proj(0, sub) * sb_scale).astype(BF16)
        _store_pair_masked(kc_ref, proj(1, sub).astype(BF16), ts, sub * (cw // LANES))
        _store_pair_masked(vc_ref, proj(2, sub).astype(BF16), ts, sub * (cw // LANES))
        gz_ref[:, out] = _silu(proj(3, sub)).astype(BF16)
        mv = proj(6, sub)
        mv_ref[:, out] = mv.astype(BF16)
        mvt_ref[out, :] = mv.T.astype(BF16)
        go_ref[:, out] = _sigmoid(proj(7, sub)).astype(BF16)
        gmz_ref[:, out] = _silu(proj(8, sub)).astype(BF16)
    gates_ref[...] = _dot_nt(wg_ref[...], u_ref[...])


def _inproj(x, pre_norm_w, w_main, w_gates, conv_w, conv_b, *, ts):
    b, s, d = x.shape
    half = d // 2
    n_gates = w_gates.shape[0]
    sb_scale = float(half // SB_HEADS) ** -0.5 * LOG2_E
    ml_scale = float(half // ML_HEADS) ** -0.5
    row = pl.BlockSpec((None, ts, half), lambda bi, i: (bi, i, 0))
    act = jax.ShapeDtypeStruct((b, s, half), BF16)
    pairs = half // LANES
    pair_blocks = pl.BlockSpec((None, pairs, ts // LANES, 2 * LANES, LANES), lambda bi, i: (bi, 0, i, 0, 0))
    pair_act = jax.ShapeDtypeStruct((b, pairs, s // LANES, 2 * LANES, LANES), BF16)
    return pl.pallas_call(
        functools.partial(_inproj_kernel, ts=ts, half=half, sb_scale=sb_scale, ml_scale=ml_scale),
        grid=(b, s // ts),
        in_specs=[
            pl.BlockSpec((None, ts, d), lambda bi, i: (bi, i, 0)),
            _const_spec((1, d)),
            _const_spec(w_main.shape),
            _const_spec(w_gates.shape),
            _const_spec(conv_w.shape),
            _const_spec((1, 2 * half)),
        ],
        out_specs=[row, pair_blocks, pair_blocks, row, row, row, row,
                   pl.BlockSpec((None, half, ts), lambda bi, i: (bi, 0, i)),
                   row, row,
                   pl.BlockSpec((None, n_gates, ts), lambda bi, i: (bi, 0, i))],
        out_shape=[act, pair_act, pair_act, act, act, act, act,
                   jax.ShapeDtypeStruct((b, half, s), BF16),
                   act, act,
                   jax.ShapeDtypeStruct((b, n_gates, s), F32)],
        scratch_shapes=[pltpu.VMEM((ts + SUBLANES, 2 * half), F32), pltpu.VMEM((ts, d), BF16)],
        compiler_params=pltpu.CompilerParams(
            dimension_semantics=("arbitrary", "arbitrary"), vmem_limit_bytes=VMEM_LIMIT_BYTES),
        name="inproj",
    )(x, pre_norm_w.reshape(1, d), w_main, w_gates, conv_w, conv_b.reshape(1, 2 * half))


def _sb_kernel(q_ref, kc_ref, vc_ref, tt_ref, y_ref, carry_ref, exit_min_ref, *, seq, pairs, unroll):
    nb = seq // LANES
    y_ref[...] = jnp.zeros_like(y_ref)
    carry_ref[...] = jnp.zeros_like(carry_ref)
    visible = (lax.broadcasted_iota(jnp.int32, (SB_WINDOW_ROWS, LANES), 1)
               < lax.broadcasted_iota(jnp.int32, (SB_WINDOW_ROWS, LANES), 0))

    def block(kb, row0, n_rows, diagonal):
        rows = pl.ds(pl.multiple_of(row0, SB_THIN_ROWS), n_rows)

        def mask_diag(x):
            return jnp.where(visible[:n_rows], x, 0.0) if diagonal else x

        for p in range(pairs):
            lanes = slice(p * LANES, (p + 1) * LANES)
            z = _dot_nt(q_ref[rows, lanes], kc_ref[p, kb])
            a_parts = []
            for hh in range(2):
                zh = z[:, hh * LANES:(hh + 1) * LANES]
                hi, lo = _split_bf16(mask_diag(_softplus2(zh)))
                ct = _dot(jnp.concatenate([hi, lo], axis=1), tt_ref[...])
                carry = carry_ref[2 * p + hh, rows, :]
                a = mask_diag(jnp.exp2(zh - (ct[:, :LANES] + carry)))
                carry_ref[2 * p + hh, rows, :] = carry + ct[:, LANES:]
                a_parts.append(a.astype(BF16))
            y_ref[rows, lanes] += _dot(jnp.concatenate(a_parts, axis=1), vc_ref[p, kb])

    def min_carry(sub):
        rows = pl.ds(pl.multiple_of(sub * LANES, LANES), LANES)
        m = carry_ref[0, rows, :]
        for hh in range(1, 2 * pairs):
            m = jnp.minimum(m, carry_ref[hh, rows, :])
        return jnp.min(m)

    block(nb - 1, (nb - 1) * LANES, LANES, True)
    block(nb - 2, (nb - 2) * LANES, 2 * LANES, True)

    def sweep(j, carry):
        for step in range(unroll):
            kb = nb - 3 - (j * unroll + step)
            block(kb, kb * LANES, SB_WINDOW_ROWS, True)
            exit_min_ref[kb + 2] = min_carry(kb + 2)
        return carry

    lax.fori_loop(0, (nb - 2) // unroll, sweep, 0)

    def finish(sub, carry):
        @pl.when(exit_min_ref[sub] < SB_DEAD_CARRY)
        def _():
            block(sub - 2, sub * LANES + SB_THIN_ROWS, LANES - SB_THIN_ROWS, False)

            def body(state):
                kb, _ = state
                block(kb, sub * LANES, LANES, False)
                return kb - 1, min_carry(sub)

            def live(state):
                kb, m = state
                return jnp.logical_and(kb >= 0, m < SB_DEAD_CARRY)

            lax.while_loop(live, body, (sub - 3, min_carry(sub)))
        return carry

    lax.fori_loop(2, nb, finish, 0)


def _sb_attention(q, kc, vc, *, pairs_per_step):
    b, s, width = q.shape
    pairs = width // LANES
    nb = s // LANES
    assert pairs % pairs_per_step == 0 and nb >= 4
    unroll = max(u for u in range(1, SB_SWEEP_UNROLL + 1) if (nb - 2) % u == 0)
    pw = pairs_per_step * LANES
    j = lax.broadcasted_iota(jnp.int32, (LANES, LANES), 0)
    t = lax.broadcasted_iota(jnp.int32, (LANES, LANES), 1)
    tri_ones = jnp.concatenate([(j >= t).astype(BF16), jnp.ones((LANES, LANES), BF16)], axis=1)
    tt = jnp.concatenate([tri_ones, tri_ones], axis=0)
    qspec = pl.BlockSpec((None, s, pw), lambda bi, p: (bi, 0, p))
    kvspec = pl.BlockSpec((None, pairs_per_step, nb, 2 * LANES, LANES), lambda bi, p: (bi, p, 0, 0, 0))
    return pl.pallas_call(
        functools.partial(_sb_kernel, seq=s, pairs=pairs_per_step, unroll=unroll),
        grid=(b, pairs // pairs_per_step),
        in_specs=[qspec, kvspec, kvspec, _const_spec(tt.shape)],
        out_specs=qspec,
        out_shape=jax.ShapeDtypeStruct((b, s, width), F32),
        scratch_shapes=[pltpu.VMEM((2 * pairs_per_step, s, LANES), F32),
                        pltpu.SMEM((nb,), F32)],
        compiler_params=pltpu.CompilerParams(
            dimension_semantics=("arbitrary", "arbitrary"), vmem_limit_bytes=VMEM_LIMIT_BYTES),
        name="sb_attention",
    )(q, kc, vc, tt)


def _mlstm_kernel(q_ref, k_ref, v_ref, vt_ref, g_ref, gb_ref, cs_ref, h_ref, st_ref, m_ref,
                  *, ts, chunk, d, nh):
    @pl.when(pl.program_id(1) == 0)
    def _():
        st_ref[...] = jnp.zeros_like(st_ref)
        m_ref[...] = jnp.zeros_like(m_ref)

    r_i = lax.broadcasted_iota(jnp.int32, (chunk, chunk), 0)
    c_i = lax.broadcasted_iota(jnp.int32, (chunk, chunk), 1)
    lower = c_i <= r_i
    ones_v = jnp.ones((chunk, d), BF16)

    states = [st_ref[h] for h in range(nh)]
    m_run = [m_ref[h, 0:1, :] for h in range(nh)]
    for c in range(ts // chunk):
        sl = slice(c * chunk, (c + 1) * chunk)
        gates = g_ref[:, sl] + gb_ref[...]
        lf_all = _log_sigmoid(gates)
        hi, lo = _split_bf16(lf_all)
        bcum = _dot(jnp.concatenate([hi, lo], axis=1), cs_ref[...])
        for h in range(nh):
            hs = slice(h * d, (h + 1) * d)
            q = q_ref[sl, hs]
            k = k_ref[sl, hs]
            v = v_ref[sl, hs]
            vt = vt_ref[hs, sl]
            ig = gates[h:h + 1, :]
            lf = lf_all[nh + h:nh + h + 1, :]
            b = bcum[nh + h:nh + h + 1, :]
            u = ig - b
            b_last = b[:, chunk - 1:chunk]
            m_loc = jnp.max(b_last + u, axis=-1, keepdims=True)
            m_prev = m_run[h]
            state = states[h]

            b_col = jnp.sum(jnp.where(lower, lf, 0.0), axis=-1, keepdims=True)
            w = jnp.where(lower, u, -jnp.inf)
            m_row = jnp.maximum(jnp.max(w, axis=-1, keepdims=True), m_prev[:, 0:1])
            scores = (_dot_nt(q, k) * jnp.exp(w - m_row)).astype(BF16)
            intra = _dot(scores, jnp.concatenate([v, ones_v], axis=1))
            inter = _dot_nt(q, state.astype(BF16))
            carry_scale = jnp.exp(m_prev[:, 0:1] - m_row)
            num = intra[:, :d] + carry_scale * inter[:, :d]
            den = intra[:, d:] + carry_scale * inter[:, d:]
            h_ref[sl, hs] = num / jnp.maximum(jnp.abs(den), jnp.exp(-(b_col + m_row)))

            w_loc = jnp.exp(b_last + u - m_loc)
            lhs = jnp.concatenate([vt.astype(F32) * w_loc, jnp.broadcast_to(w_loc, (d, chunk))], axis=0)
            upd = _dot(lhs.astype(BF16), k)
            m_new = jnp.maximum(b_last + m_prev, m_loc)
            states[h] = jnp.exp(b_last + m_prev - m_new) * state + jnp.exp(m_loc - m_new) * upd
            m_run[h] = m_new
    for h in range(nh):
        st_ref[h] = states[h]
        m_ref[h, 0:1, :] = m_run[h]


def _mlstm(q, k, v, vt, gates, gate_bias, *, ts, chunk=LANES):
    b, s, width = q.shape
    nh = ML_HEADS
    d = width // nh
    assert d == LANES and chunk == LANES
    j = lax.broadcasted_iota(jnp.int32, (chunk, chunk), 0)
    t = lax.broadcasted_iota(jnp.int32, (chunk, chunk), 1)
    triu = (j <= t).astype(BF16)
    cs = jnp.concatenate([triu, triu], axis=0)
    row = pl.BlockSpec((None, ts, width), lambda bi, i: (bi, i, 0))
    return pl.pallas_call(
        functools.partial(_mlstm_kernel, ts=ts, chunk=chunk, d=d, nh=nh),
        grid=(b, s // ts),
        in_specs=[row, row, row,
                  pl.BlockSpec((None, width, ts), lambda bi, i: (bi, 0, i)),
                  pl.BlockSpec((None, 2 * nh, ts), lambda bi, i: (bi, 0, i)),
                  _const_spec((2 * nh, 1)),
                  _const_spec(cs.shape)],
        out_specs=row,
        out_shape=jax.ShapeDtypeStruct((b, s, width), F32),
        scratch_shapes=[pltpu.VMEM((nh, 2 * d, d), F32), pltpu.VMEM((nh, SUBLANES, LANES), F32)],
        compiler_params=pltpu.CompilerParams(
            dimension_semantics=("arbitrary", "arbitrary"), vmem_limit_bytes=VMEM_LIMIT_BYTES),
        name="mlstm",
    )(q, k, v, vt, gates, gate_bias, cs)


def _out_kernel(ysb_ref, gz_ref, hml_ref, go_ref, gmz_ref, x_ref, p_ref,
                sbw_ref, mlw_ref, gsb_ref, gml_ref, wout_ref, pw_ref, wgate_ref, bgate_ref, wup_ref,
                o_ref, *, half):
    def head_mean_square(y, g_ref):
        yy = (y * y).astype(BF16)
        return jnp.concatenate([_dot(yy[:, c:c + MXU_WIDTH], g_ref[...]) for c in range(0, half, MXU_WIDTH)], axis=1)

    ysb = ysb_ref[...]
    a = (ysb * lax.rsqrt(head_mean_square(ysb, gsb_ref) + EPS)) * sbw_ref[...] * gz_ref[...].astype(F32)
    yml = go_ref[...].astype(F32) * hml_ref[...]
    bb = (yml * lax.rsqrt(head_mean_square(yml, gml_ref) + EPS)) * mlw_ref[...] * gmz_ref[...].astype(F32)
    y = _dot(a.astype(BF16), wout_ref[0:half, :]) + _dot(bb.astype(BF16), wout_ref[half:2 * half, :])
    ms_y = jnp.mean(y * y, axis=-1, keepdims=True)
    h = x_ref[...] + (y * lax.rsqrt(ms_y + EPS)) * pw_ref[...]
    gate = _sigmoid(_dot(h.astype(BF16), wgate_ref[...]) + bgate_ref[...])
    o_ref[...] = h + gate * _dot(p_ref[...].astype(BF16), wup_ref[...])


def _head_mean_matrix(hd):
    assert MXU_WIDTH % hd == 0
    j = lax.broadcasted_iota(jnp.int32, (MXU_WIDTH, MXU_WIDTH), 0) // hd
    t = lax.broadcasted_iota(jnp.int32, (MXU_WIDTH, MXU_WIDTH), 1) // hd
    return jnp.where(j == t, 1.0 / hd, 0.0).astype(BF16)


def _out_stage(ysb, gz, hml, go, gmz, x, p, sb_norm_w, ml_norm_w, w_out, post_norm_w,
               w_gate, b_gate, w_up, *, ts):
    t, d = x.shape
    half = d // 2
    pd = p.shape[-1]
    rows = lambda w: pl.BlockSpec((ts, w), lambda i: (i, 0))
    return pl.pallas_call(
        functools.partial(_out_kernel, half=half),
        grid=(t // ts,),
        in_specs=[rows(half), rows(half), rows(half), rows(half), rows(half), rows(d), rows(pd),
                  _const_spec((1, half)), _const_spec((1, half)),
                  _const_spec((MXU_WIDTH, MXU_WIDTH)), _const_spec((MXU_WIDTH, MXU_WIDTH)),
                  _const_spec((d, d)), _const_spec((1, d)),
                  _const_spec((d, d)), _const_spec((1, d)), _const_spec((pd, d))],
        out_specs=rows(d),
        out_shape=jax.ShapeDtypeStruct((t, d), F32),
        compiler_params=pltpu.CompilerParams(
            dimension_semantics=("arbitrary",), vmem_limit_bytes=VMEM_LIMIT_BYTES),
        name="out_stage",
    )(ysb, gz, hml, go, gmz, x, p,
      sb_norm_w.reshape(1, half), ml_norm_w.reshape(1, half),
      _head_mean_matrix(half // SB_HEADS), _head_mean_matrix(half // ML_HEADS),
      w_out, post_norm_w.reshape(1, d), w_gate, b_gate.reshape(1, d), w_up)


def _layer(h, p, pre_norm_w, w_in, ml_conv_w, ml_conv_b, ml_i_bias, ml_f_bias, sb_norm_w, ml_norm_w,
           w_out, post_norm_w, ple_w_up, ple_w_gate, ple_b_gate):
    b, s, d = h.shape
    half = d // 2
    tiles = _tiles(s)
    wb = w_in.astype(BF16)
    n_main = 9 * half
    q, kc, vc, gz, mq, mk, mv, mvt, go, gmz, gates = _inproj(
        h, pre_norm_w, wb, wb[:, n_main:].T,
        ml_conv_w, ml_conv_b, ts=tiles["inproj"])
    ysb = _sb_attention(q, kc, vc, pairs_per_step=tiles["sb_pairs"])
    gate_bias = jnp.concatenate([ml_i_bias, ml_f_bias]).reshape(2 * ML_HEADS, 1)
    hml = _mlstm(mq, mk, mv, mvt, gates, gate_bias, ts=tiles["ml"])
    flat = lambda a: a.reshape(b * s, a.shape[-1])
    out = _out_stage(flat(ysb), flat(gz), flat(hml), flat(go), flat(gmz), flat(h), flat(p),
                     sb_norm_w, ml_norm_w, w_out.astype(BF16), post_norm_w,
                     ple_w_gate.astype(BF16), ple_b_gate, ple_w_up.astype(BF16), ts=tiles["out"])
    return out.reshape(b, s, d)


def kernel(x, p, pre_norm_w, w_in, ml_conv_w, ml_conv_b, ml_i_bias, ml_f_bias, sb_norm_w, ml_norm_w,
           w_out, post_norm_w, ple_w_up, ple_w_gate, ple_b_gate):
    h = x
    for i in range(pre_norm_w.shape[0]):
        h = _layer(h, p[i], pre_norm_w[i], w_in[i], ml_conv_w[i], ml_conv_b[i], ml_i_bias[i], ml_f_bias[i],
                   sb_norm_w[i], ml_norm_w[i], w_out[i], post_norm_w[i], ple_w_up[i], ple_w_gate[i],
                   ple_b_gate[i])
    return h
```

```python
import functools

import jax
import jax.numpy as jnp
from jax import lax
from jax.experimental import pallas as pl
from jax.experimental.pallas import tpu as pltpu

SB_HEADS = 8
ML_HEADS = 4
CONV_K = 4
EPS = 1e-6
LOG2_E = 1.4426950408889634

LANES = 128
SUBLANES = 8
MXU_WIDTH = 256
VMEM_LIMIT_BYTES = 56 * 1024 * 1024

SB_DEAD_CARRY = 152.0
SB_THIN_ROWS = 32
SB_WINDOW_ROWS = 2 * LANES + SB_THIN_ROWS
SB_SWEEP_UNROLL = 10

F32 = jnp.float32
BF16 = jnp.bfloat16


def _tiles(seq):
    def pick(pref):
        t = min(pref, seq)
        assert seq % t == 0 and t % LANES == 0
        return t
    return dict(inproj=pick(1024), sb_pairs=2, ml=pick(1024), out=pick(1024))


def _dot(a, b):
    return jnp.dot(a, b, preferred_element_type=F32)


def _dot_nt(a, b):
    return lax.dot_general(a, b, (((1,), (1,)), ((), ())), preferred_element_type=F32)


def _split_bf16(x):
    hi = x.astype(BF16)
    lo = (x - hi.astype(F32)).astype(BF16)
    return hi, lo


def _sigmoid(x):
    return 0.5 + 0.5 * jnp.tanh(0.5 * x)


def _silu(x):
    h = 0.5 * x
    return h + h * jnp.tanh(h)


def _softplus(x):
    return jnp.maximum(x, 0.0) + jnp.log(1.0 + jnp.exp(-jnp.abs(x)))


def _log_sigmoid(x):
    return -_softplus(-x)


def _softplus2(x):
    return jnp.maximum(x, 0.0) + jnp.log2(1.0 + jnp.exp2(-jnp.abs(x)))


def _const_spec(shape):
    return pl.BlockSpec(shape, lambda *_: (0,) * len(shape), pipeline_mode=pl.Buffered(1))


def _store_pair_masked(dst_ref, a, ts, pair0):
    first_head = lax.broadcasted_iota(jnp.int32, (LANES, LANES), 1) < LANES // 2
    zero = jnp.zeros((LANES, LANES), a.dtype)
    for p in range(a.shape[1] // LANES):
        for j in range(ts // LANES):
            blk = a[j * LANES:(j + 1) * LANES, p * LANES:(p + 1) * LANES]
            dst_ref[pair0 + p, j, 0:LANES, :] = jnp.where(first_head, blk, zero)
            dst_ref[pair0 + p, j, LANES:2 * LANES, :] = jnp.where(first_head, zero, blk)


def _inproj_kernel(x_ref, pnw_ref, w_ref, wg_ref, cw_ref, cb_ref,
                   q_ref, kc_ref, vc_ref, gz_ref, mq_ref, mk_ref, mv_ref, mvt_ref,
                   go_ref, gmz_ref, gates_ref, halo_ref, u_ref, *, ts, half, sb_scale, ml_scale):
    @pl.when(pl.program_id(1) == 0)
    def _():
        halo_ref[...] = jnp.zeros_like(halo_ref)

    x = x_ref[...]
    ms = jnp.mean(x * x, axis=-1, keepdims=True)
    u_ref[...] = ((x * lax.rsqrt(ms + EPS)) * pnw_ref[...]).astype(BF16)

    cw = MXU_WIDTH
    per_seg = half // cw

    def proj(seg, sub):
        col = seg * half + sub * cw
        return _dot(u_ref[...], w_ref[:, col:col + cw])

    def conv_silu(seg, sub):
        cols = slice((seg - 4) * half + sub * cw, (seg - 4) * half + (sub + 1) * cw)
        cur = proj(seg, sub)
        full = jnp.concatenate([halo_ref[:, cols], cur], axis=0)
        halo_ref[:, cols] = cur[ts - SUBLANES:ts]
        acc = cb_ref[:, cols]
        for j in range(CONV_K):
            delay = CONV_K - 1 - j
            tap = full if delay == 0 else pltpu.roll(full, delay, axis=0)
            acc = acc + cw_ref[j:j + 1, cols] * tap[SUBLANES:SUBLANES + ts]
        return _silu(acc)

    for sub in range(per_seg):
        out = slice(sub * cw, (sub + 1) * cw)
        mq_ref[:, out] = conv_silu(4, sub).astype(BF16)
        mk_ref[:, out] = (conv_silu(5, sub) * ml_scale).astype(BF16)
        q_ref[:, out] = (proj(0, sub) * sb_scale).astype(BF16)
        _store_pair_masked(kc_ref, proj(1, sub).astype(BF16), ts, sub * (cw // LANES))
        _store_pair_masked(vc_ref, proj(2, sub).astype(BF16), ts, sub * (cw // LANES))
        gz_ref[:, out] = _silu(proj(3, sub)).astype(BF16)
        mv = proj(6, sub)
        mv_ref[:, out] = mv.astype(BF16)
        mvt_ref[out, :] = mv.T.astype(BF16)
        go_ref[:, out] = _sigmoid(proj(7, sub)).astype(BF16)
        gmz_ref[:, out] = _silu(proj(8, sub)).astype(BF16)
    gates_ref[...] = _dot_nt(wg_ref[...], u_ref[...])


def _inproj(x, pre_norm_w, w_main, w_gates, conv_w, conv_b, *, ts):
    b, s, d = x.shape
    half = d // 2
    n_gates = w_gates.shape[0]
    sb_scale = float(half // SB_HEADS) ** -0.5 * LOG2_E
    ml_scale = float(half // ML_HEADS) ** -0.5
    row = pl.BlockSpec((None, ts, half), lambda bi, i: (bi, i, 0))
    act = jax.ShapeDtypeStruct((b, s, half), BF16)
    pairs = half // LANES
    pair_blocks = pl.BlockSpec((None, pairs, ts // LANES, 2 * LANES, LANES), lambda bi, i: (bi, 0, i, 0, 0))
    pair_act = jax.ShapeDtypeStruct((b, pairs, s // LANES, 2 * LANES, LANES), BF16)
    return pl.pallas_call(
        functools.partial(_inproj_kernel, ts=ts, half=half, sb_scale=sb_scale, ml_scale=ml_scale),
        grid=(b, s // ts),
        in_specs=[
            pl.BlockSpec((None, ts, d), lambda bi, i: (bi, i, 0)),
            _const_spec((1, d)),
            _const_spec(w_main.shape),
            _const_spec(w_gates.shape),
            _const_spec(conv_w.shape),
            _const_spec((1, 2 * half)),
        ],
        out_specs=[row, pair_blocks, pair_blocks, row, row, row, row,
                   pl.BlockSpec((None, half, ts), lambda bi, i: (bi, 0, i)),
                   row, row,
                   pl.BlockSpec((None, n_gates, ts), lambda bi, i: (bi, 0, i))],
        out_shape=[act, pair_act, pair_act, act, act, act, act,
                   jax.ShapeDtypeStruct((b, half, s), BF16),
                   act, act,
                   jax.ShapeDtypeStruct((b, n_gates, s), F32)],
        scratch_shapes=[pltpu.VMEM((SUBLANES, 2 * half), F32), pltpu.VMEM((ts, d), BF16)],
        compiler_params=pltpu.CompilerParams(
            dimension_semantics=("arbitrary", "arbitrary"), vmem_limit_bytes=VMEM_LIMIT_BYTES),
        name="inproj",
    )(x, pre_norm_w.reshape(1, d), w_main, w_gates, conv_w, conv_b.reshape(1, 2 * half))


def _sb_kernel(q_ref, kc_ref, vc_ref, tt_ref, y_ref, carry_ref, exit_min_ref, *, seq, pairs, unroll):
    nb = seq // LANES
    y_ref[...] = jnp.zeros_like(y_ref)
    carry_ref[...] = jnp.zeros_like(carry_ref)
    visible = (lax.broadcasted_iota(jnp.int32, (SB_WINDOW_ROWS, LANES), 1)
               < lax.broadcasted_iota(jnp.int32, (SB_WINDOW_ROWS, LANES), 0))

    def block(kb, row0, n_rows, diagonal):
        rows = pl.ds(pl.multiple_of(row0, SB_THIN_ROWS), n_rows)

        def mask_diag(x):
            return jnp.where(visible[:n_rows], x, 0.0) if diagonal else x

        for p in range(pairs):
            lanes = slice(p * LANES, (p + 1) * LANES)
            z = _dot_nt(q_ref[rows, lanes], kc_ref[p, kb])
            a_parts = []
            for hh in range(2):
                zh = z[:, hh * LANES:(hh + 1) * LANES]
                hi, lo = _split_bf16(mask_diag(_softplus2(zh)))
                ct = _dot(jnp.concatenate([hi, lo], axis=1), tt_ref[...])
                carry = carry_ref[2 * p + hh, rows, :]
                a = mask_diag(jnp.exp2(zh - (ct[:, :LANES] + carry)))
                carry_ref[2 * p + hh, rows, :] = carry + ct[:, LANES:]
                a_parts.append(a.astype(BF16))
            y_ref[rows, lanes] += _dot(jnp.concatenate(a_parts, axis=1), vc_ref[p, kb])

    def min_carry(sub):
        rows = pl.ds(pl.multiple_of(sub * LANES, LANES), LANES)
        m = carry_ref[0, rows, :]
        for hh in range(1, 2 * pairs):
            m = jnp.minimum(m, carry_ref[hh, rows, :])
        return jnp.min(m)

    block(nb - 1, (nb - 1) * LANES, LANES, True)
    block(nb - 2, (nb - 2) * LANES, 2 * LANES, True)

    def sweep(j, carry):
        for step in range(unroll):
            kb = nb - 3 - (j * unroll + step)
            block(kb, kb * LANES, SB_WINDOW_ROWS, True)
            exit_min_ref[kb + 2] = min_carry(kb + 2)
        return carry

    lax.fori_loop(0, (nb - 2) // unroll, sweep, 0)

    def finish(sub, carry):
        @pl.when(exit_min_ref[sub] < SB_DEAD_CARRY)
        def _():
            block(sub - 2, sub * LANES + SB_THIN_ROWS, LANES - SB_THIN_ROWS, False)

            def body(state):
                kb, _ = state
                block(kb, sub * LANES, LANES, False)
                return kb - 1, min_carry(sub)

            def live(state):
                kb, m = state
                return jnp.logical_and(kb >= 0, m < SB_DEAD_CARRY)

            lax.while_loop(live, body, (sub - 3, min_carry(sub)))
        return carry

    lax.fori_loop(2, nb, finish, 0)


def _sb_attention(q, kc, vc, *, pairs_per_step):
    b, s, width = q.shape
    pairs = width // LANES
    nb = s // LANES
    assert pairs % pairs_per_step == 0 and nb >= 4
    unroll = max(u for u in range(1, SB_SWEEP_UNROLL + 1) if (nb - 2) % u == 0)
    pw = pairs_per_step * LANES
    j = lax.broadcasted_iota(jnp.int32, (LANES, LANES), 0)
    t = lax.broadcasted_iota(jnp.int32, (LANES, LANES), 1)
    tri_ones = jnp.concatenate([(j >= t).astype(BF16), jnp.ones((LANES, LANES), BF16)], axis=1)
    tt = jnp.concatenate([tri_ones, tri_ones], axis=0)
    qspec = pl.BlockSpec((None, s, pw), lambda bi, p: (bi, 0, p))
    kvspec = pl.BlockSpec((None, pairs_per_step, nb, 2 * LANES, LANES), lambda bi, p: (bi, p, 0, 0, 0))
    return pl.pallas_call(
        functools.partial(_sb_kernel, seq=s, pairs=pairs_per_step, unroll=unroll),
        grid=(b, pairs // pairs_per_step),
        in_specs=[qspec, kvspec, kvspec, _const_spec(tt.shape)],
        out_specs=qspec,
        out_shape=jax.ShapeDtypeStruct((b, s, width), F32),
        scratch_shapes=[pltpu.VMEM((2 * pairs_per_step, s, LANES), F32),
                        pltpu.SMEM((nb,), F32)],
        compiler_params=pltpu.CompilerParams(
            dimension_semantics=("arbitrary", "arbitrary"), vmem_limit_bytes=VMEM_LIMIT_BYTES),
        name="sb_attention",
    )(q, kc, vc, tt)


def _mlstm_kernel(q_ref, k_ref, v_ref, vt_ref, g_ref, gb_ref, cs_ref, h_ref, st_ref, m_ref,
                  *, ts, chunk, d, nh):
    @pl.when(pl.program_id(1) == 0)
    def _():
        st_ref[...] = jnp.zeros_like(st_ref)
        m_ref[...] = jnp.zeros_like(m_ref)

    r_i = lax.broadcasted_iota(jnp.int32, (chunk, chunk), 0)
    c_i = lax.broadcasted_iota(jnp.int32, (chunk, chunk), 1)
    lower = c_i <= r_i
    ones_v = jnp.ones((chunk, d), BF16)

    states = [st_ref[h] for h in range(nh)]
    m_run = [m_ref[h, 0:1, :] for h in range(nh)]
    for c in range(ts // chunk):
        sl = slice(c * chunk, (c + 1) * chunk)
        gates = g_ref[:, sl] + gb_ref[...]
        lf_all = _log_sigmoid(gates)
        hi, lo = _split_bf16(lf_all)
        bcum = _dot(jnp.concatenate([hi, lo], axis=1), cs_ref[...])
        for h in range(nh):
            hs = slice(h * d, (h + 1) * d)
            q = q_ref[sl, hs]
            k = k_ref[sl, hs]
            v = v_ref[sl, hs]
            vt = vt_ref[hs, sl]
            ig = gates[h:h + 1, :]
            lf = lf_all[nh + h:nh + h + 1, :]
            b = bcum[nh + h:nh + h + 1, :]
            u = ig - b
            b_last = b[:, chunk - 1:chunk]
            m_loc = jnp.max(b_last + u, axis=-1, keepdims=True)
            m_prev = m_run[h]
            state = states[h]

            b_col = jnp.sum(jnp.where(lower, lf, 0.0), axis=-1, keepdims=True)
            w = jnp.where(lower, u, -jnp.inf)
            m_row = jnp.maximum(jnp.max(w, axis=-1, keepdims=True), m_prev[:, 0:1])
            scores = (_dot_nt(q, k) * jnp.exp(w - m_row)).astype(BF16)
            intra = _dot(scores, jnp.concatenate([v, ones_v], axis=1))
            inter = _dot_nt(q, state.astype(BF16))
            carry_scale = jnp.exp(m_prev[:, 0:1] - m_row)
            num = intra[:, :d] + carry_scale * inter[:, :d]
            den = intra[:, d:] + carry_scale * inter[:, d:]
            h_ref[sl, hs] = num / jnp.maximum(jnp.abs(den), jnp.exp(-(b_col + m_row)))

            w_loc = jnp.exp(b_last + u - m_loc)
            lhs = jnp.concatenate([vt.astype(F32) * w_loc, jnp.broadcast_to(w_loc, (d, chunk))], axis=0)
            upd = _dot(lhs.astype(BF16), k)
            m_new = jnp.maximum(b_last + m_prev, m_loc)
            states[h] = jnp.exp(b_last + m_prev - m_new) * state + jnp.exp(m_loc - m_new) * upd
            m_run[h] = m_new
    for h in range(nh):
        st_ref[h] = states[h]
        m_ref[h, 0:1, :] = m_run[h]


def _mlstm(q, k, v, vt, gates, gate_bias, *, ts, chunk=LANES):
    b, s, width = q.shape
    nh = ML_HEADS
    d = width // nh
    assert d == LANES and chunk == LANES
    j = lax.broadcasted_iota(jnp.int32, (chunk, chunk), 0)
    t = lax.broadcasted_iota(jnp.int32, (chunk, chunk), 1)
    triu = (j <= t).astype(BF16)
    cs = jnp.concatenate([triu, triu], axis=0)
    row = pl.BlockSpec((None, ts, width), lambda bi, i: (bi, i, 0))
    return pl.pallas_call(
        functools.partial(_mlstm_kernel, ts=ts, chunk=chunk, d=d, nh=nh),
        grid=(b, s // ts),
        in_specs=[row, row, row,
                  pl.BlockSpec((None, width, ts), lambda bi, i: (bi, 0, i)),
                  pl.BlockSpec((None, 2 * nh, ts), lambda bi, i: (bi, 0, i)),
                  _const_spec((2 * nh, 1)),
                  _const_spec(cs.shape)],
        out_specs=row,
        out_shape=jax.ShapeDtypeStruct((b, s, width), F32),
        scratch_shapes=[pltpu.VMEM((nh, 2 * d, d), F32), pltpu.VMEM((nh, SUBLANES, LANES), F32)],
        compiler_params=pltpu.CompilerParams(
            dimension_semantics=("arbitrary", "arbitrary"), vmem_limit_bytes=VMEM_LIMIT_BYTES),
        name="mlstm",
    )(q, k, v, vt, gates, gate_bias, cs)


def _out_kernel(ysb_ref, gz_ref, hml_ref, go_ref, gmz_ref, x_ref, p_ref,
                sbw_ref, mlw_ref, gsb_ref, gml_ref, wout_ref, pw_ref, wgate_ref, bgate_ref, wup_ref,
                o_ref, *, half):
    def head_mean_square(y, g_ref):
        yy = (y * y).astype(BF16)
        return jnp.concatenate([_dot(yy[:, c:c + MXU_WIDTH], g_ref[...]) for c in range(0, half, MXU_WIDTH)], axis=1)

    ysb = ysb_ref[...]
    a = (ysb * lax.rsqrt(head_mean_square(ysb, gsb_ref) + EPS)) * sbw_ref[...] * gz_ref[...].astype(F32)
    yml = go_ref[...].astype(F32) * hml_ref[...]
    bb = (yml * lax.rsqrt(head_mean_square(yml, gml_ref) + EPS)) * mlw_ref[...] * gmz_ref[...].astype(F32)
    y = _dot(a.astype(BF16), wout_ref[0:half, :]) + _dot(bb.astype(BF16), wout_ref[half:2 * half, :])
    ms_y = jnp.mean(y * y, axis=-1, keepdims=True)
    h = x_ref[...] + (y * lax.rsqrt(ms_y + EPS)) * pw_ref[...]
    gate = _sigmoid(_dot(h.astype(BF16), wgate_ref[...]) + bgate_ref[...])
    o_ref[...] = h + gate * _dot(p_ref[...].astype(BF16), wup_ref[...])


def _head_mean_matrix(hd):
    assert MXU_WIDTH % hd == 0
    j = lax.broadcasted_iota(jnp.int32, (MXU_WIDTH, MXU_WIDTH), 0) // hd
    t = lax.broadcasted_iota(jnp.int32, (MXU_WIDTH, MXU_WIDTH), 1) // hd
    return jnp.where(j == t, 1.0 / hd, 0.0).astype(BF16)


def _out_stage(ysb, gz, hml, go, gmz, x, p, sb_norm_w, ml_norm_w, w_out, post_norm_w,
               w_gate, b_gate, w_up, *, ts):
    t, d = x.shape
    half = d // 2
    pd = p.shape[-1]
    rows = lambda w: pl.BlockSpec((ts, w), lambda i: (i, 0))
    return pl.pallas_call(
        functools.partial(_out_kernel, half=half),
        grid=(t // ts,),
        in_specs=[rows(half), rows(half), rows(half), rows(half), rows(half), rows(d), rows(pd),
                  _const_spec((1, half)), _const_spec((1, half)),
                  _const_spec((MXU_WIDTH, MXU_WIDTH)), _const_spec((MXU_WIDTH, MXU_WIDTH)),
                  _const_spec((d, d)), _const_spec((1, d)),
                  _const_spec((d, d)), _const_spec((1, d)), _const_spec((pd, d))],
        out_specs=rows(d),
        out_shape=jax.ShapeDtypeStruct((t, d), F32),
        compiler_params=pltpu.CompilerParams(
            dimension_semantics=("arbitrary",), vmem_limit_bytes=VMEM_LIMIT_BYTES),
        name="out_stage",
    )(ysb, gz, hml, go, gmz, x, p,
      sb_norm_w.reshape(1, half), ml_norm_w.reshape(1, half),
      _head_mean_matrix(half // SB_HEADS), _head_mean_matrix(half // ML_HEADS),
      w_out, post_norm_w.reshape(1, d), w_gate, b_gate.reshape(1, d), w_up)


def _layer(h, p, pre_norm_w, w_in, ml_conv_w, ml_conv_b, ml_i_bias, ml_f_bias, sb_norm_w, ml_norm_w,
           w_out, post_norm_w, ple_w_up, ple_w_gate, ple_b_gate):
    b, s, d = h.shape
    half = d // 2
    tiles = _tiles(s)
    n_main = 9 * half
    q, kc, vc, gz, mq, mk, mv, mvt, go, gmz, gates = _inproj(
        h, pre_norm_w, w_in[:, :n_main].astype(BF16), w_in[:, n_main:].T.astype(BF16),
        ml_conv_w, ml_conv_b, ts=tiles["inproj"])
    ysb = _sb_attention(q, kc, vc, pairs_per_step=tiles["sb_pairs"])
    gate_bias = jnp.concatenate([ml_i_bias, ml_f_bias]).reshape(2 * ML_HEADS, 1)
    hml = _mlstm(mq, mk, mv, mvt, gates, gate_bias, ts=tiles["ml"])
    flat = lambda a: a.reshape(b * s, a.shape[-1])
    out = _out_stage(flat(ysb), flat(gz), flat(hml), flat(go), flat(gmz), flat(h), flat(p),
                     sb_norm_w, ml_norm_w, w_out.astype(BF16), post_norm_w,
                     ple_w_gate.astype(BF16), ple_b_gate, ple_w_up.astype(BF16), ts=tiles["out"])
    return out.reshape(b, s, d)


def kernel(x, p, pre_norm_w, w_in, ml_conv_w, ml_conv_b, ml_i_bias, ml_f_bias, sb_norm_w, ml_norm_w,
           w_out, post_norm_w, ple_w_up, ple_w_gate, ple_b_gate):
    h = x
    for i in range(pre_norm_w.shape[0]):
        h = _layer(h, p[i], pre_norm_w[i], w_in[i], ml_conv_w[i], ml_conv_b[i], ml_i_bias[i], ml_f_bias[i],
                   sb_norm_w[i], ml_norm_w[i], w_out[i], post_norm_w[i], ple_w_up[i], ple_w_gate[i],
                   ple_b_gate[i])
    return h
```

```python
import functools

import jax
import jax.numpy as jnp
from jax import lax
from jax.experimental import pallas as pl
from jax.experimental.pallas import tpu as pltpu

SB_HEADS = 8
ML_HEADS = 4
CONV_K = 4
EPS = 1e-6
LOG2_E = 1.4426950408889634

LANES = 128
SUBLANES = 8
MXU_WIDTH = 256
VMEM_LIMIT_BYTES = 56 * 1024 * 1024

SB_DEAD_CARRY = 152.0
SB_THIN_ROWS = 32
SB_WINDOW_ROWS = 2 * LANES + SB_THIN_ROWS
SB_SWEEP_UNROLL = 10

F32 = jnp.float32
BF16 = jnp.bfloat16


def _tiles(seq):
    def pick(pref):
        t = min(pref, seq)
        assert seq % t == 0 and t % LANES == 0
        return t
    return dict(inproj=pick(1024), sb_pairs=2, ml=pick(1024), out=pick(1024))


def _dot(a, b):
    return jnp.dot(a, b, preferred_element_type=F32)


def _dot_nt(a, b):
    return lax.dot_general(a, b, (((1,), (1,)), ((), ())), preferred_element_type=F32)


def _split_bf16(x):
    hi = x.astype(BF16)
    lo = (x - hi.astype(F32)).astype(BF16)
    return hi, lo


def _sigmoid(x):
    return 0.5 + 0.5 * jnp.tanh(0.5 * x)


def _silu(x):
    h = 0.5 * x
    return h + h * jnp.tanh(h)


def _softplus(x):
    return jnp.maximum(x, 0.0) + jnp.log(1.0 + jnp.exp(-jnp.abs(x)))


def _log_sigmoid(x):
    return -_softplus(-x)


def _softplus2(x):
    return jnp.maximum(x, 0.0) + jnp.log2(1.0 + jnp.exp2(-jnp.abs(x)))


def _const_spec(shape):
    return pl.BlockSpec(shape, lambda *_: (0,) * len(shape), pipeline_mode=pl.Buffered(1))


def _store_pair_masked(dst_ref, a, ts, pair0):
    first_head = lax.broadcasted_iota(jnp.int32, (LANES, LANES), 1) < LANES // 2
    zero = jnp.zeros((LANES, LANES), a.dtype)
    for p in range(a.shape[1] // LANES):
        for j in range(ts // LANES):
            blk = a[j * LANES:(j + 1) * LANES, p * LANES:(p + 1) * LANES]
            dst_ref[pair0 + p, j, 0:LANES, :] = jnp.where(first_head, blk, zero)
            dst_ref[pair0 + p, j, LANES:2 * LANES, :] = jnp.where(first_head, zero, blk)


def _inproj_kernel(x_ref, pnw_ref, wt_ref, cw_ref, cb_ref,
                   q_ref, kc_ref, vc_ref, gz_ref, mq_ref, mk_ref, mv_ref, mvt_ref,
                   go_ref, gmz_ref, gates_ref, halo_ref, u_ref, *, ts, half, sb_scale, ml_scale):
    @pl.when(pl.program_id(1) == 0)
    def _():
        halo_ref[...] = jnp.zeros_like(halo_ref)

    x = x_ref[...]
    ms = jnp.mean(x * x, axis=-1, keepdims=True)
    u_ref[...] = ((x * lax.rsqrt(ms + EPS)) * pnw_ref[...]).astype(BF16)

    cw = MXU_WIDTH
    per_seg = half // cw

    def proj(seg, sub):
        col = seg * half + sub * cw
        return _dot_nt(u_ref[...], wt_ref[col:col + cw, :])

    def conv_silu(seg, sub):
        cols = slice((seg - 4) * half + sub * cw, (seg - 4) * half + (sub + 1) * cw)
        cur = proj(seg, sub)
        full = jnp.concatenate([halo_ref[:, cols], cur], axis=0)
        halo_ref[:, cols] = cur[ts - SUBLANES:ts]
        acc = cb_ref[:, cols]
        for j in range(CONV_K):
            delay = CONV_K - 1 - j
            tap = full if delay == 0 else pltpu.roll(full, delay, axis=0)
            acc = acc + cw_ref[j:j + 1, cols] * tap[SUBLANES:SUBLANES + ts]
        return _silu(acc)

    for sub in range(per_seg):
        out = slice(sub * cw, (sub + 1) * cw)
        mq_ref[:, out] = conv_silu(4, sub).astype(BF16)
        mk_ref[:, out] = (conv_silu(5, sub) * ml_scale).astype(BF16)
        q_ref[:, out] = (proj(0, sub) * sb_scale).astype(BF16)
        _store_pair_masked(kc_ref, proj(1, sub).astype(BF16), ts, sub * (cw // LANES))
        _store_pair_masked(vc_ref, proj(2, sub).astype(BF16), ts, sub * (cw // LANES))
        gz_ref[:, out] = _silu(proj(3, sub)).astype(BF16)
        mv = proj(6, sub)
        mv_ref[:, out] = mv.astype(BF16)
        mvt_ref[out, :] = mv.T.astype(BF16)
        go_ref[:, out] = _sigmoid(proj(7, sub)).astype(BF16)
        gmz_ref[:, out] = _silu(proj(8, sub)).astype(BF16)
    n_main = 9 * half
    gates_ref[...] = _dot_nt(wt_ref[n_main:, :], u_ref[...])


def _inproj(x, pre_norm_w, w_t, conv_w, conv_b, *, ts):
    b, s, d = x.shape
    half = d // 2
    n_gates = w_t.shape[0] - 9 * half
    sb_scale = float(half // SB_HEADS) ** -0.5 * LOG2_E
    ml_scale = float(half // ML_HEADS) ** -0.5
    row = pl.BlockSpec((None, ts, half), lambda bi, i: (bi, i, 0))
    act = jax.ShapeDtypeStruct((b, s, half), BF16)
    pairs = half // LANES
    pair_blocks = pl.BlockSpec((None, pairs, ts // LANES, 2 * LANES, LANES), lambda bi, i: (bi, 0, i, 0, 0))
    pair_act = jax.ShapeDtypeStruct((b, pairs, s // LANES, 2 * LANES, LANES), BF16)
    return pl.pallas_call(
        functools.partial(_inproj_kernel, ts=ts, half=half, sb_scale=sb_scale, ml_scale=ml_scale),
        grid=(b, s // ts),
        in_specs=[
            pl.BlockSpec((None, ts, d), lambda bi, i: (bi, i, 0)),
            _const_spec((1, d)),
            _const_spec(w_t.shape),
            _const_spec(conv_w.shape),
            _const_spec((1, 2 * half)),
        ],
        out_specs=[row, pair_blocks, pair_blocks, row, row, row, row,
                   pl.BlockSpec((None, half, ts), lambda bi, i: (bi, 0, i)),
                   row, row,
                   pl.BlockSpec((None, n_gates, ts), lambda bi, i: (bi, 0, i))],
        out_shape=[act, pair_act, pair_act, act, act, act, act,
                   jax.ShapeDtypeStruct((b, half, s), BF16),
                   act, act,
                   jax.ShapeDtypeStruct((b, n_gates, s), F32)],
        scratch_shapes=[pltpu.VMEM((SUBLANES, 2 * half), F32), pltpu.VMEM((ts, d), BF16)],
        compiler_params=pltpu.CompilerParams(
            dimension_semantics=("arbitrary", "arbitrary"), vmem_limit_bytes=VMEM_LIMIT_BYTES),
        name="inproj",
    )(x, pre_norm_w.reshape(1, d), w_t, conv_w, conv_b.reshape(1, 2 * half))


def _sb_kernel(q_ref, kc_ref, vc_ref, tt_ref, y_ref, carry_ref, exit_min_ref, *, seq, pairs, unroll):
    nb = seq // LANES
    y_ref[...] = jnp.zeros_like(y_ref)
    carry_ref[...] = jnp.zeros_like(carry_ref)
    visible = (lax.broadcasted_iota(jnp.int32, (SB_WINDOW_ROWS, LANES), 1)
               < lax.broadcasted_iota(jnp.int32, (SB_WINDOW_ROWS, LANES), 0))

    def block(kb, row0, n_rows, diagonal):
        rows = pl.ds(pl.multiple_of(row0, SB_THIN_ROWS), n_rows)

        def mask_diag(x):
            return jnp.where(visible[:n_rows], x, 0.0) if diagonal else x

        for p in range(pairs):
            lanes = slice(p * LANES, (p + 1) * LANES)
            z = _dot_nt(q_ref[rows, lanes], kc_ref[p, kb])
            a_parts = []
            for hh in range(2):
                zh = z[:, hh * LANES:(hh + 1) * LANES]
                hi, lo = _split_bf16(mask_diag(_softplus2(zh)))
                ct = _dot(jnp.concatenate([hi, lo], axis=1), tt_ref[...])
                carry = carry_ref[2 * p + hh, rows, :]
                a = mask_diag(jnp.exp2(zh - (ct[:, :LANES] + carry)))
                carry_ref[2 * p + hh, rows, :] = carry + ct[:, LANES:]
                a_parts.append(a.astype(BF16))
            y_ref[rows, lanes] += _dot(jnp.concatenate(a_parts, axis=1), vc_ref[p, kb])

    def min_carry(sub):
        rows = pl.ds(pl.multiple_of(sub * LANES, LANES), LANES)
        m = carry_ref[0, rows, :]
        for hh in range(1, 2 * pairs):
            m = jnp.minimum(m, carry_ref[hh, rows, :])
        return jnp.min(m)

    block(nb - 1, (nb - 1) * LANES, LANES, True)
    block(nb - 2, (nb - 2) * LANES, 2 * LANES, True)

    def sweep(j, carry):
        for step in range(unroll):
            kb = nb - 3 - (j * unroll + step)
            block(kb, kb * LANES, SB_WINDOW_ROWS, True)
            exit_min_ref[kb + 2] = min_carry(kb + 2)
        return carry

    lax.fori_loop(0, (nb - 2) // unroll, sweep, 0)

    def finish(sub, carry):
        @pl.when(exit_min_ref[sub] < SB_DEAD_CARRY)
        def _():
            block(sub - 2, sub * LANES + SB_THIN_ROWS, LANES - SB_THIN_ROWS, False)

            def body(state):
                kb, _ = state
                block(kb, sub * LANES, LANES, False)
                return kb - 1, min_carry(sub)

            def live(state):
                kb, m = state
                return jnp.logical_and(kb >= 0, m < SB_DEAD_CARRY)

            lax.while_loop(live, body, (sub - 3, min_carry(sub)))
        return carry

    lax.fori_loop(2, nb, finish, 0)


def _sb_attention(q, kc, vc, *, pairs_per_step):
    b, s, width = q.shape
    pairs = width // LANES
    nb = s // LANES
    assert pairs % pairs_per_step == 0 and nb >= 4
    unroll = max(u for u in range(1, SB_SWEEP_UNROLL + 1) if (nb - 2) % u == 0)
    pw = pairs_per_step * LANES
    j = lax.broadcasted_iota(jnp.int32, (LANES, LANES), 0)
    t = lax.broadcasted_iota(jnp.int32, (LANES, LANES), 1)
    tri_ones = jnp.concatenate([(j >= t).astype(BF16), jnp.ones((LANES, LANES), BF16)], axis=1)
    tt = jnp.concatenate([tri_ones, tri_ones], axis=0)
    qspec = pl.BlockSpec((None, s, pw), lambda bi, p: (bi, 0, p))
    kvspec = pl.BlockSpec((None, pairs_per_step, nb, 2 * LANES, LANES), lambda bi, p: (bi, p, 0, 0, 0))
    return pl.pallas_call(
        functools.partial(_sb_kernel, seq=s, pairs=pairs_per_step, unroll=unroll),
        grid=(b, pairs // pairs_per_step),
        in_specs=[qspec, kvspec, kvspec, _const_spec(tt.shape)],
        out_specs=qspec,
        out_shape=jax.ShapeDtypeStruct((b, s, width), F32),
        scratch_shapes=[pltpu.VMEM((2 * pairs_per_step, s, LANES), F32),
                        pltpu.SMEM((nb,), F32)],
        compiler_params=pltpu.CompilerParams(
            dimension_semantics=("arbitrary", "arbitrary"), vmem_limit_bytes=VMEM_LIMIT_BYTES),
        name="sb_attention",
    )(q, kc, vc, tt)


def _mlstm_kernel(q_ref, k_ref, v_ref, vt_ref, g_ref, gb_ref, cs_ref, h_ref, st_ref, m_ref,
                  *, ts, chunk, d, nh):
    @pl.when(pl.program_id(1) == 0)
    def _():
        st_ref[...] = jnp.zeros_like(st_ref)
        m_ref[...] = jnp.zeros_like(m_ref)

    r_i = lax.broadcasted_iota(jnp.int32, (chunk, chunk), 0)
    c_i = lax.broadcasted_iota(jnp.int32, (chunk, chunk), 1)
    lower = c_i <= r_i
    ones_v = jnp.ones((chunk, d), BF16)

    states = [st_ref[h] for h in range(nh)]
    m_run = [m_ref[h, 0:1, :] for h in range(nh)]
    for c in range(ts // chunk):
        sl = slice(c * chunk, (c + 1) * chunk)
        gates = g_ref[:, sl] + gb_ref[...]
        lf_all = _log_sigmoid(gates)
        hi, lo = _split_bf16(lf_all)
        bcum = _dot(jnp.concatenate([hi, lo], axis=1), cs_ref[...])
        for h in range(nh):
            hs = slice(h * d, (h + 1) * d)
            q = q_ref[sl, hs]
            k = k_ref[sl, hs]
            v = v_ref[sl, hs]
            vt = vt_ref[hs, sl]
            ig = gates[h:h + 1, :]
            lf = lf_all[nh + h:nh + h + 1, :]
            b = bcum[nh + h:nh + h + 1, :]
            u = ig - b
            b_last = b[:, chunk - 1:chunk]
            m_loc = jnp.max(b_last + u, axis=-1, keepdims=True)
            m_prev = m_run[h]
            state = states[h]

            b_col = jnp.sum(jnp.where(lower, lf, 0.0), axis=-1, keepdims=True)
            w = jnp.where(lower, u, -jnp.inf)
            m_row = jnp.maximum(jnp.max(w, axis=-1, keepdims=True), m_prev[:, 0:1])
            scores = (_dot_nt(q, k) * jnp.exp(w - m_row)).astype(BF16)
            intra = _dot(scores, jnp.concatenate([v, ones_v], axis=1))
            inter = _dot_nt(q, state.astype(BF16))
            carry_scale = jnp.exp(m_prev[:, 0:1] - m_row)
            num = intra[:, :d] + carry_scale * inter[:, :d]
            den = intra[:, d:] + carry_scale * inter[:, d:]
            h_ref[sl, hs] = num / jnp.maximum(jnp.abs(den), jnp.exp(-(b_col + m_row)))

            w_loc = jnp.exp(b_last + u - m_loc)
            lhs = jnp.concatenate([vt.astype(F32) * w_loc, jnp.broadcast_to(w_loc, (d, chunk))], axis=0)
            upd = _dot(lhs.astype(BF16), k)
            m_new = jnp.maximum(b_last + m_prev, m_loc)
            states[h] = jnp.exp(b_last + m_prev - m_new) * state + jnp.exp(m_loc - m_new) * upd
            m_run[h] = m_new
    for h in range(nh):
        st_ref[h] = states[h]
        m_ref[h, 0:1, :] = m_run[h]


def _mlstm(q, k, v, vt, gates, gate_bias, *, ts, chunk=LANES):
    b, s, width = q.shape
    nh = ML_HEADS
    d = width // nh
    assert d == LANES and chunk == LANES
    j = lax.broadcasted_iota(jnp.int32, (chunk, chunk), 0)
    t = lax.broadcasted_iota(jnp.int32, (chunk, chunk), 1)
    triu = (j <= t).astype(BF16)
    cs = jnp.concatenate([triu, triu], axis=0)
    row = pl.BlockSpec((None, ts, width), lambda bi, i: (bi, i, 0))
    return pl.pallas_call(
        functools.partial(_mlstm_kernel, ts=ts, chunk=chunk, d=d, nh=nh),
        grid=(b, s // ts),
        in_specs=[row, row, row,
                  pl.BlockSpec((None, width, ts), lambda bi, i: (bi, 0, i)),
                  pl.BlockSpec((None, 2 * nh, ts), lambda bi, i: (bi, 0, i)),
                  _const_spec((2 * nh, 1)),
                  _const_spec(cs.shape)],
        out_specs=row,
        out_shape=jax.ShapeDtypeStruct((b, s, width), F32),
        scratch_shapes=[pltpu.VMEM((nh, 2 * d, d), F32), pltpu.VMEM((nh, SUBLANES, LANES), F32)],
        compiler_params=pltpu.CompilerParams(
            dimension_semantics=("arbitrary", "arbitrary"), vmem_limit_bytes=VMEM_LIMIT_BYTES),
        name="mlstm",
    )(q, k, v, vt, gates, gate_bias, cs)


def _out_kernel(ysb_ref, gz_ref, hml_ref, go_ref, gmz_ref, x_ref, p_ref,
                sbw_ref, mlw_ref, gsb_ref, gml_ref, wout_ref, pw_ref, wgate_ref, bgate_ref, wup_ref,
                o_ref, *, half):
    def head_mean_square(y, g_ref):
        yy = (y * y).astype(BF16)
        return jnp.concatenate([_dot(yy[:, c:c + MXU_WIDTH], g_ref[...]) for c in range(0, half, MXU_WIDTH)], axis=1)

    ysb = ysb_ref[...]
    a = (ysb * lax.rsqrt(head_mean_square(ysb, gsb_ref) + EPS)) * sbw_ref[...] * gz_ref[...].astype(F32)
    yml = go_ref[...].astype(F32) * hml_ref[...]
    bb = (yml * lax.rsqrt(head_mean_square(yml, gml_ref) + EPS)) * mlw_ref[...] * gmz_ref[...].astype(F32)
    y = _dot(a.astype(BF16), wout_ref[0:half, :]) + _dot(bb.astype(BF16), wout_ref[half:2 * half, :])
    ms_y = jnp.mean(y * y, axis=-1, keepdims=True)
    h = x_ref[...] + (y * lax.rsqrt(ms_y + EPS)) * pw_ref[...]
    gate = _sigmoid(_dot(h.astype(BF16), wgate_ref[...]) + bgate_ref[...])
    o_ref[...] = h + gate * _dot(p_ref[...].astype(BF16), wup_ref[...])


def _head_mean_matrix(hd):
    assert MXU_WIDTH % hd == 0
    j = lax.broadcasted_iota(jnp.int32, (MXU_WIDTH, MXU_WIDTH), 0) // hd
    t = lax.broadcasted_iota(jnp.int32, (MXU_WIDTH, MXU_WIDTH), 1) // hd
    return jnp.where(j == t, 1.0 / hd, 0.0).astype(BF16)


def _out_stage(ysb, gz, hml, go, gmz, x, p, sb_norm_w, ml_norm_w, w_out, post_norm_w,
               w_gate, b_gate, w_up, *, ts):
    t, d = x.shape
    half = d // 2
    pd = p.shape[-1]
    rows = lambda w: pl.BlockSpec((ts, w), lambda i: (i, 0))
    return pl.pallas_call(
        functools.partial(_out_kernel, half=half),
        grid=(t // ts,),
        in_specs=[rows(half), rows(half), rows(half), rows(half), rows(half), rows(d), rows(pd),
                  _const_spec((1, half)), _const_spec((1, half)),
                  _const_spec((MXU_WIDTH, MXU_WIDTH)), _const_spec((MXU_WIDTH, MXU_WIDTH)),
                  _const_spec((d, d)), _const_spec((1, d)),
                  _const_spec((d, d)), _const_spec((1, d)), _const_spec((pd, d))],
        out_specs=rows(d),
        out_shape=jax.ShapeDtypeStruct((t, d), F32),
        compiler_params=pltpu.CompilerParams(
            dimension_semantics=("arbitrary",), vmem_limit_bytes=VMEM_LIMIT_BYTES),
        name="out_stage",
    )(ysb, gz, hml, go, gmz, x, p,
      sb_norm_w.reshape(1, half), ml_norm_w.reshape(1, half),
      _head_mean_matrix(half // SB_HEADS), _head_mean_matrix(half // ML_HEADS),
      w_out, post_norm_w.reshape(1, d), w_gate, b_gate.reshape(1, d), w_up)


def _layer(h, p, pre_norm_w, w_in, ml_conv_w, ml_conv_b, ml_i_bias, ml_f_bias, sb_norm_w, ml_norm_w,
           w_out, post_norm_w, ple_w_up, ple_w_gate, ple_b_gate):
    b, s, d = h.shape
    half = d // 2
    tiles = _tiles(s)
    q, kc, vc, gz, mq, mk, mv, mvt, go, gmz, gates = _inproj(
        h, pre_norm_w, w_in.T.astype(BF16), ml_conv_w, ml_conv_b, ts=tiles["inproj"])
    ysb = _sb_attention(q, kc, vc, pairs_per_step=tiles["sb_pairs"])
    gate_bias = jnp.concatenate([ml_i_bias, ml_f_bias]).reshape(2 * ML_HEADS, 1)
    hml = _mlstm(mq, mk, mv, mvt, gates, gate_bias, ts=tiles["ml"])
    flat = lambda a: a.reshape(b * s, a.shape[-1])
    out = _out_stage(flat(ysb), flat(gz), flat(hml), flat(go), flat(gmz), flat(h), flat(p),
                     sb_norm_w, ml_norm_w, w_out.astype(BF16), post_norm_w,
                     ple_w_gate.astype(BF16), ple_b_gate, ple_w_up.astype(BF16), ts=tiles["out"])
    return out.reshape(b, s, d)


def kernel(x, p, pre_norm_w, w_in, ml_conv_w, ml_conv_b, ml_i_bias, ml_f_bias, sb_norm_w, ml_norm_w,
           w_out, post_norm_w, ple_w_up, ple_w_gate, ple_b_gate):
    h = x
    for i in range(pre_norm_w.shape[0]):
        h = _layer(h, p[i], pre_norm_w[i], w_in[i], ml_conv_w[i], ml_conv_b[i], ml_i_bias[i], ml_f_bias[i],
                   sb_norm_w[i], ml_norm_w[i], w_out[i], post_norm_w[i], ple_w_up[i], ple_w_gate[i],
                   ple_b_gate[i])
    return h
```

```python
import functools

import jax
import jax.numpy as jnp
from jax import lax
from jax.experimental import pallas as pl
from jax.experimental.pallas import tpu as pltpu

SB_HEADS = 8
ML_HEADS = 4
CONV_K = 4
EPS = 1e-6
LOG2_E = 1.4426950408889634

LANES = 128
SUBLANES = 8
MXU_WIDTH = 256
VMEM_LIMIT_BYTES = 56 * 1024 * 1024

SB_DEAD_CARRY = 152.0
SB_THIN_ROWS = 32
SB_WINDOW_ROWS = 2 * LANES + SB_THIN_ROWS
SB_SWEEP_UNROLL = 10

F32 = jnp.float32
BF16 = jnp.bfloat16


def _tiles(seq):
    def pick(pref):
        t = min(pref, seq)
        assert seq % t == 0 and t % LANES == 0
        return t
    return dict(inproj=pick(1024), sb_pairs=2, ml=pick(1024), out=pick(1024))


def _dot(a, b):
    return jnp.dot(a, b, preferred_element_type=F32)


def _dot_nt(a, b):
    return lax.dot_general(a, b, (((1,), (1,)), ((), ())), preferred_element_type=F32)


def _split_bf16(x):
    hi = x.astype(BF16)
    lo = (x - hi.astype(F32)).astype(BF16)
    return hi, lo


def _sigmoid(x):
    return 0.5 + 0.5 * jnp.tanh(0.5 * x)


def _silu(x):
    h = 0.5 * x
    return h + h * jnp.tanh(h)


def _softplus(x):
    return jnp.maximum(x, 0.0) + jnp.log(1.0 + jnp.exp(-jnp.abs(x)))


def _log_sigmoid(x):
    return -_softplus(-x)


def _softplus2(x):
    return jnp.maximum(x, 0.0) + jnp.log2(1.0 + jnp.exp2(-jnp.abs(x)))


def _const_spec(shape):
    return pl.BlockSpec(shape, lambda *_: (0,) * len(shape), pipeline_mode=pl.Buffered(1))


def _store_pair_masked(dst_ref, a, ts, pair0):
    first_head = lax.broadcasted_iota(jnp.int32, (LANES, LANES), 1) < LANES // 2
    zero = jnp.zeros((LANES, LANES), a.dtype)
    for p in range(a.shape[1] // LANES):
        for j in range(ts // LANES):
            blk = a[j * LANES:(j + 1) * LANES, p * LANES:(p + 1) * LANES]
            dst_ref[pair0 + p, j, 0:LANES, :] = jnp.where(first_head, blk, zero)
            dst_ref[pair0 + p, j, LANES:2 * LANES, :] = jnp.where(first_head, zero, blk)


def _inproj_kernel(x_ref, pnw_ref, wt_ref, cw_ref, cb_ref,
                   q_ref, kc_ref, vc_ref, gz_ref, mq_ref, mk_ref, mv_ref, mvt_ref,
                   go_ref, gmz_ref, gates_ref, halo_ref, u_ref, *, ts, half, sb_scale, ml_scale):
    @pl.when(pl.program_id(1) == 0)
    def _():
        halo_ref[...] = jnp.zeros_like(halo_ref)

    x = x_ref[...]
    ms = jnp.mean(x * x, axis=-1, keepdims=True)
    u_ref[...] = ((x * lax.rsqrt(ms + EPS)) * pnw_ref[...]).astype(BF16)

    cw = MXU_WIDTH
    per_seg = half // cw

    def proj(seg, sub):
        col = seg * half + sub * cw
        return _dot_nt(u_ref[...], wt_ref[col:col + cw, :])

    def conv_silu(seg, sub):
        cols = slice((seg - 4) * half + sub * cw, (seg - 4) * half + (sub + 1) * cw)
        cur = proj(seg, sub)
        full = jnp.concatenate([halo_ref[:, cols], cur], axis=0)
        halo_ref[:, cols] = cur[ts - SUBLANES:ts]
        acc = cb_ref[:, cols]
        for j in range(CONV_K):
            delay = CONV_K - 1 - j
            tap = full if delay == 0 else pltpu.roll(full, delay, axis=0)
            acc = acc + cw_ref[j:j + 1, cols] * tap[SUBLANES:SUBLANES + ts]
        return _silu(acc)

    for sub in range(per_seg):
        out = slice(sub * cw, (sub + 1) * cw)
        q_ref[:, out] = (proj(0, sub) * sb_scale).astype(BF16)
        mq_ref[:, out] = conv_silu(4, sub).astype(BF16)
        _store_pair_masked(kc_ref, proj(1, sub).astype(BF16), ts, sub * (cw // LANES))
        _store_pair_masked(vc_ref, proj(2, sub).astype(BF16), ts, sub * (cw // LANES))
        gz_ref[:, out] = _silu(proj(3, sub)).astype(BF16)
        mv = proj(6, sub)
        mv_ref[:, out] = mv.astype(BF16)
        mvt_ref[out, :] = mv.T.astype(BF16)
        mk_ref[:, out] = (conv_silu(5, sub) * ml_scale).astype(BF16)
        go_ref[:, out] = _sigmoid(proj(7, sub)).astype(BF16)
        gmz_ref[:, out] = _silu(proj(8, sub)).astype(BF16)
    n_main = 9 * half
    gates_ref[...] = _dot_nt(wt_ref[n_main:, :], u_ref[...])


def _inproj(x, pre_norm_w, w_t, conv_w, conv_b, *, ts):
    b, s, d = x.shape
    half = d // 2
    n_gates = w_t.shape[0] - 9 * half
    sb_scale = float(half // SB_HEADS) ** -0.5 * LOG2_E
    ml_scale = float(half // ML_HEADS) ** -0.5
    row = pl.BlockSpec((None, ts, half), lambda bi, i: (bi, i, 0))
    act = jax.ShapeDtypeStruct((b, s, half), BF16)
    pairs = half // LANES
    pair_blocks = pl.BlockSpec((None, pairs, ts // LANES, 2 * LANES, LANES), lambda bi, i: (bi, 0, i, 0, 0))
    pair_act = jax.ShapeDtypeStruct((b, pairs, s // LANES, 2 * LANES, LANES), BF16)
    return pl.pallas_call(
        functools.partial(_inproj_kernel, ts=ts, half=half, sb_scale=sb_scale, ml_scale=ml_scale),
        grid=(b, s // ts),
        in_specs=[
            pl.BlockSpec((None, ts, d), lambda bi, i: (bi, i, 0)),
            _const_spec((1, d)),
            _const_spec(w_t.shape),
            _const_spec(conv_w.shape),
            _const_spec((1, 2 * half)),
        ],
        out_specs=[row, pair_blocks, pair_blocks, row, row, row, row,
                   pl.BlockSpec((None, half, ts), lambda bi, i: (bi, 0, i)),
                   row, row,
                   pl.BlockSpec((None, n_gates, ts), lambda bi, i: (bi, 0, i))],
        out_shape=[act, pair_act, pair_act, act, act, act, act,
                   jax.ShapeDtypeStruct((b, half, s), BF16),
                   act, act,
                   jax.ShapeDtypeStruct((b, n_gates, s), F32)],
        scratch_shapes=[pltpu.VMEM((SUBLANES, 2 * half), F32), pltpu.VMEM((ts, d), BF16)],
        compiler_params=pltpu.CompilerParams(
            dimension_semantics=("arbitrary", "arbitrary"), vmem_limit_bytes=VMEM_LIMIT_BYTES),
        name="inproj",
    )(x, pre_norm_w.reshape(1, d), w_t, conv_w, conv_b.reshape(1, 2 * half))


def _sb_kernel(q_ref, kc_ref, vc_ref, tt_ref, y_ref, carry_ref, exit_min_ref, *, seq, pairs, unroll):
    nb = seq // LANES
    y_ref[...] = jnp.zeros_like(y_ref)
    carry_ref[...] = jnp.zeros_like(carry_ref)
    visible = (lax.broadcasted_iota(jnp.int32, (SB_WINDOW_ROWS, LANES), 1)
               < lax.broadcasted_iota(jnp.int32, (SB_WINDOW_ROWS, LANES), 0))

    def block(kb, row0, n_rows, diagonal):
        rows = pl.ds(pl.multiple_of(row0, SB_THIN_ROWS), n_rows)

        def mask_diag(x):
            return jnp.where(visible[:n_rows], x, 0.0) if diagonal else x

        for p in range(pairs):
            lanes = slice(p * LANES, (p + 1) * LANES)
            z = _dot_nt(q_ref[rows, lanes], kc_ref[p, kb])
            a_parts = []
            for hh in range(2):
                zh = z[:, hh * LANES:(hh + 1) * LANES]
                hi, lo = _split_bf16(mask_diag(_softplus2(zh)))
                ct = _dot(jnp.concatenate([hi, lo], axis=1), tt_ref[...])
                carry = carry_ref[2 * p + hh, rows, :]
                a = mask_diag(jnp.exp2(zh - (ct[:, :LANES] + carry)))
                carry_ref[2 * p + hh, rows, :] = carry + ct[:, LANES:]
                a_parts.append(a.astype(BF16))
            y_ref[rows, lanes] += _dot(jnp.concatenate(a_parts, axis=1), vc_ref[p, kb])

    def min_carry(sub):
        rows = pl.ds(pl.multiple_of(sub * LANES, LANES), LANES)
        m = carry_ref[0, rows, :]
        for hh in range(1, 2 * pairs):
            m = jnp.minimum(m, carry_ref[hh, rows, :])
        return jnp.min(m)

    block(nb - 1, (nb - 1) * LANES, LANES, True)
    block(nb - 2, (nb - 2) * LANES, 2 * LANES, True)

    def sweep(j, carry):
        for step in range(unroll):
            kb = nb - 3 - (j * unroll + step)
            block(kb, kb * LANES, SB_WINDOW_ROWS, True)
            exit_min_ref[kb + 2] = min_carry(kb + 2)
        return carry

    lax.fori_loop(0, (nb - 2) // unroll, sweep, 0)

    def finish(sub, carry):
        @pl.when(exit_min_ref[sub] < SB_DEAD_CARRY)
        def _():
            block(sub - 2, sub * LANES + SB_THIN_ROWS, LANES - SB_THIN_ROWS, False)

            def body(state):
                kb, _ = state
                block(kb, sub * LANES, LANES, False)
                return kb - 1, min_carry(sub)

            def live(state):
                kb, m = state
                return jnp.logical_and(kb >= 0, m < SB_DEAD_CARRY)

            lax.while_loop(live, body, (sub - 3, min_carry(sub)))
        return carry

    lax.fori_loop(2, nb, finish, 0)


def _sb_attention(q, kc, vc, *, pairs_per_step):
    b, s, width = q.shape
    pairs = width // LANES
    nb = s // LANES
    assert pairs % pairs_per_step == 0 and nb >= 4
    unroll = max(u for u in range(1, SB_SWEEP_UNROLL + 1) if (nb - 2) % u == 0)
    pw = pairs_per_step * LANES
    j = lax.broadcasted_iota(jnp.int32, (LANES, LANES), 0)
    t = lax.broadcasted_iota(jnp.int32, (LANES, LANES), 1)
    tri_ones = jnp.concatenate([(j >= t).astype(BF16), jnp.ones((LANES, LANES), BF16)], axis=1)
    tt = jnp.concatenate([tri_ones, tri_ones], axis=0)
    qspec = pl.BlockSpec((None, s, pw), lambda bi, p: (bi, 0, p))
    kvspec = pl.BlockSpec((None, pairs_per_step, nb, 2 * LANES, LANES), lambda bi, p: (bi, p, 0, 0, 0))
    return pl.pallas_call(
        functools.partial(_sb_kernel, seq=s, pairs=pairs_per_step, unroll=unroll),
        grid=(b, pairs // pairs_per_step),
        in_specs=[qspec, kvspec, kvspec, _const_spec(tt.shape)],
        out_specs=qspec,
        out_shape=jax.ShapeDtypeStruct((b, s, width), F32),
        scratch_shapes=[pltpu.VMEM((2 * pairs_per_step, s, LANES), F32),
                        pltpu.SMEM((nb,), F32)],
        compiler_params=pltpu.CompilerParams(
            dimension_semantics=("arbitrary", "arbitrary"), vmem_limit_bytes=VMEM_LIMIT_BYTES),
        name="sb_attention",
    )(q, kc, vc, tt)


def _mlstm_kernel(q_ref, k_ref, v_ref, vt_ref, g_ref, gb_ref, cs_ref, h_ref, st_ref, m_ref,
                  *, ts, chunk, d, nh):
    @pl.when(pl.program_id(1) == 0)
    def _():
        st_ref[...] = jnp.zeros_like(st_ref)
        m_ref[...] = jnp.zeros_like(m_ref)

    r_i = lax.broadcasted_iota(jnp.int32, (chunk, chunk), 0)
    c_i = lax.broadcasted_iota(jnp.int32, (chunk, chunk), 1)
    lower = c_i <= r_i
    ones_v = jnp.ones((chunk, d), BF16)

    states = [st_ref[h] for h in range(nh)]
    m_run = [m_ref[h, 0:1, :] for h in range(nh)]
    for c in range(ts // chunk):
        sl = slice(c * chunk, (c + 1) * chunk)
        gates = g_ref[:, sl] + gb_ref[...]
        lf_all = _log_sigmoid(gates)
        hi, lo = _split_bf16(lf_all)
        bcum = _dot(jnp.concatenate([hi, lo], axis=1), cs_ref[...])
        for h in range(nh):
            hs = slice(h * d, (h + 1) * d)
            q = q_ref[sl, hs]
            k = k_ref[sl, hs]
            v = v_ref[sl, hs]
            vt = vt_ref[hs, sl]
            ig = gates[h:h + 1, :]
            lf = lf_all[nh + h:nh + h + 1, :]
            b = bcum[nh + h:nh + h + 1, :]
            u = ig - b
            b_last = b[:, chunk - 1:chunk]
            m_loc = jnp.max(b_last + u, axis=-1, keepdims=True)
            m_prev = m_run[h]
            state = states[h]

            b_col = jnp.sum(jnp.where(lower, lf, 0.0), axis=-1, keepdims=True)
            w = jnp.where(lower, u, -jnp.inf)
            m_row = jnp.maximum(jnp.max(w, axis=-1, keepdims=True), m_prev[:, 0:1])
            scores = (_dot_nt(q, k) * jnp.exp(w - m_row)).astype(BF16)
            intra = _dot(scores, jnp.concatenate([v, ones_v], axis=1))
            inter = _dot_nt(q, state.astype(BF16))
            carry_scale = jnp.exp(m_prev[:, 0:1] - m_row)
            num = intra[:, :d] + carry_scale * inter[:, :d]
            den = intra[:, d:] + carry_scale * inter[:, d:]
            h_ref[sl, hs] = num / jnp.maximum(jnp.abs(den), jnp.exp(-(b_col + m_row)))

            w_loc = jnp.exp(b_last + u - m_loc)
            lhs = jnp.concatenate([vt.astype(F32) * w_loc, jnp.broadcast_to(w_loc, (d, chunk))], axis=0)
            upd = _dot(lhs.astype(BF16), k)
            m_new = jnp.maximum(b_last + m_prev, m_loc)
            states[h] = jnp.exp(b_last + m_prev - m_new) * state + jnp.exp(m_loc - m_new) * upd
            m_run[h] = m_new
    for h in range(nh):
        st_ref[h] = states[h]
        m_ref[h, 0:1, :] = m_run[h]


def _mlstm(q, k, v, vt, gates, gate_bias, *, ts, chunk=LANES):
    b, s, width = q.shape
    nh = ML_HEADS
    d = width // nh
    assert d == LANES and chunk == LANES
    j = lax.broadcasted_iota(jnp.int32, (chunk, chunk), 0)
    t = lax.broadcasted_iota(jnp.int32, (chunk, chunk), 1)
    triu = (j <= t).astype(BF16)
    cs = jnp.concatenate([triu, triu], axis=0)
    row = pl.BlockSpec((None, ts, width), lambda bi, i: (bi, i, 0))
    return pl.pallas_call(
        functools.partial(_mlstm_kernel, ts=ts, chunk=chunk, d=d, nh=nh),
        grid=(b, s // ts),
        in_specs=[row, row, row,
                  pl.BlockSpec((None, width, ts), lambda bi, i: (bi, 0, i)),
                  pl.BlockSpec((None, 2 * nh, ts), lambda bi, i: (bi, 0, i)),
                  _const_spec((2 * nh, 1)),
                  _const_spec(cs.shape)],
        out_specs=row,
        out_shape=jax.ShapeDtypeStruct((b, s, width), F32),
        scratch_shapes=[pltpu.VMEM((nh, 2 * d, d), F32), pltpu.VMEM((nh, SUBLANES, LANES), F32)],
        compiler_params=pltpu.CompilerParams(
            dimension_semantics=("arbitrary", "arbitrary"), vmem_limit_bytes=VMEM_LIMIT_BYTES),
        name="mlstm",
    )(q, k, v, vt, gates, gate_bias, cs)


def _out_kernel(ysb_ref, gz_ref, hml_ref, go_ref, gmz_ref, x_ref, p_ref,
                sbw_ref, mlw_ref, gsb_ref, gml_ref, wout_ref, pw_ref, wgate_ref, bgate_ref, wup_ref,
                o_ref, *, half):
    def head_mean_square(y, g_ref):
        yy = (y * y).astype(BF16)
        return jnp.concatenate([_dot(yy[:, c:c + MXU_WIDTH], g_ref[...]) for c in range(0, half, MXU_WIDTH)], axis=1)

    ysb = ysb_ref[...]
    a = (ysb * lax.rsqrt(head_mean_square(ysb, gsb_ref) + EPS)) * sbw_ref[...] * gz_ref[...].astype(F32)
    yml = go_ref[...].astype(F32) * hml_ref[...]
    bb = (yml * lax.rsqrt(head_mean_square(yml, gml_ref) + EPS)) * mlw_ref[...] * gmz_ref[...].astype(F32)
    y = _dot(a.astype(BF16), wout_ref[0:half, :]) + _dot(bb.astype(BF16), wout_ref[half:2 * half, :])
    ms_y = jnp.mean(y * y, axis=-1, keepdims=True)
    h = x_ref[...] + (y * lax.rsqrt(ms_y + EPS)) * pw_ref[...]
    gate = _sigmoid(_dot(h.astype(BF16), wgate_ref[...]) + bgate_ref[...])
    o_ref[...] = h + gate * _dot(p_ref[...].astype(BF16), wup_ref[...])


def _head_mean_matrix(hd):
    assert MXU_WIDTH % hd == 0
    j = lax.broadcasted_iota(jnp.int32, (MXU_WIDTH, MXU_WIDTH), 0) // hd
    t = lax.broadcasted_iota(jnp.int32, (MXU_WIDTH, MXU_WIDTH), 1) // hd
    return jnp.where(j == t, 1.0 / hd, 0.0).astype(BF16)


def _out_stage(ysb, gz, hml, go, gmz, x, p, sb_norm_w, ml_norm_w, w_out, post_norm_w,
               w_gate, b_gate, w_up, *, ts):
    t, d = x.shape
    half = d // 2
    pd = p.shape[-1]
    rows = lambda w: pl.BlockSpec((ts, w), lambda i: (i, 0))
    return pl.pallas_call(
        functools.partial(_out_kernel, half=half),
        grid=(t // ts,),
        in_specs=[rows(half), rows(half), rows(half), rows(half), rows(half), rows(d), rows(pd),
                  _const_spec((1, half)), _const_spec((1, half)),
                  _const_spec((MXU_WIDTH, MXU_WIDTH)), _const_spec((MXU_WIDTH, MXU_WIDTH)),
                  _const_spec((d, d)), _const_spec((1, d)),
                  _const_spec((d, d)), _const_spec((1, d)), _const_spec((pd, d))],
        out_specs=rows(d),
        out_shape=jax.ShapeDtypeStruct((t, d), F32),
        compiler_params=pltpu.CompilerParams(
            dimension_semantics=("arbitrary",), vmem_limit_bytes=VMEM_LIMIT_BYTES),
        name="out_stage",
    )(ysb, gz, hml, go, gmz, x, p,
      sb_norm_w.reshape(1, half), ml_norm_w.reshape(1, half),
      _head_mean_matrix(half // SB_HEADS), _head_mean_matrix(half // ML_HEADS),
      w_out, post_norm_w.reshape(1, d), w_gate, b_gate.reshape(1, d), w_up)


def _layer(h, p, pre_norm_w, w_in, ml_conv_w, ml_conv_b, ml_i_bias, ml_f_bias, sb_norm_w, ml_norm_w,
           w_out, post_norm_w, ple_w_up, ple_w_gate, ple_b_gate):
    b, s, d = h.shape
    half = d // 2
    tiles = _tiles(s)
    q, kc, vc, gz, mq, mk, mv, mvt, go, gmz, gates = _inproj(
        h, pre_norm_w, w_in.T.astype(BF16), ml_conv_w, ml_conv_b, ts=tiles["inproj"])
    ysb = _sb_attention(q, kc, vc, pairs_per_step=tiles["sb_pairs"])
    gate_bias = jnp.concatenate([ml_i_bias, ml_f_bias]).reshape(2 * ML_HEADS, 1)
    hml = _mlstm(mq, mk, mv, mvt, gates, gate_bias, ts=tiles["ml"])
    flat = lambda a: a.reshape(b * s, a.shape[-1])
    out = _out_stage(flat(ysb), flat(gz), flat(hml), flat(go), flat(gmz), flat(h), flat(p),
                     sb_norm_w, ml_norm_w, w_out.astype(BF16), post_norm_w,
                     ple_w_gate.astype(BF16), ple_b_gate, ple_w_up.astype(BF16), ts=tiles["out"])
    return out.reshape(b, s, d)


def kernel(x, p, pre_norm_w, w_in, ml_conv_w, ml_conv_b, ml_i_bias, ml_f_bias, sb_norm_w, ml_norm_w,
           w_out, post_norm_w, ple_w_up, ple_w_gate, ple_b_gate):
    h = x
    for i in range(pre_norm_w.shape[0]):
        h = _layer(h, p[i], pre_norm_w[i], w_in[i], ml_conv_w[i], ml_conv_b[i], ml_i_bias[i], ml_f_bias[i],
                   sb_norm_w[i], ml_norm_w[i], w_out[i], post_norm_w[i], ple_w_up[i], ple_w_gate[i],
                   ple_b_gate[i])
    return h
```

```python
import functools

import jax
import jax.numpy as jnp
from jax import lax
from jax.experimental import pallas as pl
from jax.experimental.pallas import tpu as pltpu

SB_HEADS = 8
ML_HEADS = 4
CONV_K = 4
EPS = 1e-6
LOG2_E = 1.4426950408889634

LANES = 128
SUBLANES = 8
MXU_WIDTH = 256
VMEM_LIMIT_BYTES = 56 * 1024 * 1024

SB_DEAD_CARRY = 152.0
SB_THIN_ROWS = 32
SB_WINDOW_ROWS = 2 * LANES + SB_THIN_ROWS
SB_SWEEP_UNROLL = 10

F32 = jnp.float32
BF16 = jnp.bfloat16


def _tiles(seq):
    def pick(pref):
        t = min(pref, seq)
        assert seq % t == 0 and t % LANES == 0
        return t
    return dict(inproj=pick(1024), sb_pairs=2, ml=pick(1024), out=pick(1024))


def _dot(a, b):
    return jnp.dot(a, b, preferred_element_type=F32)


def _dot_nt(a, b):
    return lax.dot_general(a, b, (((1,), (1,)), ((), ())), preferred_element_type=F32)


def _split_bf16(x):
    hi = x.astype(BF16)
    lo = (x - hi.astype(F32)).astype(BF16)
    return hi, lo


def _sigmoid(x):
    return 0.5 + 0.5 * jnp.tanh(0.5 * x)


def _silu(x):
    h = 0.5 * x
    return h + h * jnp.tanh(h)


def _softplus(x):
    return jnp.maximum(x, 0.0) + jnp.log(1.0 + jnp.exp(-jnp.abs(x)))


def _log_sigmoid(x):
    return -_softplus(-x)


def _softplus2(x):
    return jnp.maximum(x, 0.0) + jnp.log2(1.0 + jnp.exp2(-jnp.abs(x)))


def _const_spec(shape):
    return pl.BlockSpec(shape, lambda *_: (0,) * len(shape), pipeline_mode=pl.Buffered(1))


def _store_pair_masked(dst_ref, a, ts, pair0):
    first_head = lax.broadcasted_iota(jnp.int32, (LANES, LANES), 1) < LANES // 2
    zero = jnp.zeros((LANES, LANES), a.dtype)
    for p in range(a.shape[1] // LANES):
        for j in range(ts // LANES):
            blk = a[j * LANES:(j + 1) * LANES, p * LANES:(p + 1) * LANES]
            dst_ref[pair0 + p, j, 0:LANES, :] = jnp.where(first_head, blk, zero)
            dst_ref[pair0 + p, j, LANES:2 * LANES, :] = jnp.where(first_head, zero, blk)


def _inproj_kernel(x_ref, pnw_ref, wt_ref, cw_ref, cb_ref,
                   q_ref, kc_ref, vc_ref, gz_ref, mq_ref, mk_ref, mv_ref, mvt_ref,
                   go_ref, gmz_ref, gates_ref, halo_ref, u_ref, *, ts, half, sb_scale, ml_scale):
    @pl.when(pl.program_id(1) == 0)
    def _():
        halo_ref[...] = jnp.zeros_like(halo_ref)

    x = x_ref[...]
    ms = jnp.mean(x * x, axis=-1, keepdims=True)
    u_ref[...] = ((x * lax.rsqrt(ms + EPS)) * pnw_ref[...]).astype(BF16)

    cw = MXU_WIDTH
    per_seg = half // cw

    def proj(seg, sub):
        col = seg * half + sub * cw
        return _dot_nt(u_ref[...], wt_ref[col:col + cw, :])

    def conv_silu(seg, sub):
        cols = slice((seg - 4) * half + sub * cw, (seg - 4) * half + (sub + 1) * cw)
        cur = proj(seg, sub)
        full = jnp.concatenate([halo_ref[:, cols], cur], axis=0)
        halo_ref[:, cols] = cur[ts - SUBLANES:ts]
        acc = cb_ref[:, cols]
        for j in range(CONV_K):
            delay = CONV_K - 1 - j
            tap = full if delay == 0 else pltpu.roll(full, delay, axis=0)
            acc = acc + cw_ref[j:j + 1, cols] * tap[SUBLANES:SUBLANES + ts]
        return _silu(acc)

    for sub in range(per_seg):
        out = slice(sub * cw, (sub + 1) * cw)
        mq_ref[:, out] = conv_silu(4, sub).astype(BF16)
        mk_ref[:, out] = (conv_silu(5, sub) * ml_scale).astype(BF16)
        q_ref[:, out] = (proj(0, sub) * sb_scale).astype(BF16)
        _store_pair_masked(kc_ref, proj(1, sub).astype(BF16), ts, sub * (cw // LANES))
        _store_pair_masked(vc_ref, proj(2, sub).astype(BF16), ts, sub * (cw // LANES))
        gz_ref[:, out] = _silu(proj(3, sub)).astype(BF16)
        mv = proj(6, sub)
        mv_ref[:, out] = mv.astype(BF16)
        mvt_ref[out, :] = mv.T.astype(BF16)
        go_ref[:, out] = _sigmoid(proj(7, sub)).astype(BF16)
        gmz_ref[:, out] = _silu(proj(8, sub)).astype(BF16)
    n_main = 9 * half
    gates_ref[...] = _dot_nt(wt_ref[n_main:, :], u_ref[...])


def _inproj(x, pre_norm_w, w_t, conv_w, conv_b, *, ts):
    b, s, d = x.shape
    half = d // 2
    n_gates = w_t.shape[0] - 9 * half
    sb_scale = float(half // SB_HEADS) ** -0.5 * LOG2_E
    ml_scale = float(half // ML_HEADS) ** -0.5
    row = pl.BlockSpec((None, ts, half), lambda bi, i: (bi, i, 0))
    act = jax.ShapeDtypeStruct((b, s, half), BF16)
    pairs = half // LANES
    pair_blocks = pl.BlockSpec((None, pairs, ts // LANES, 2 * LANES, LANES), lambda bi, i: (bi, 0, i, 0, 0))
    pair_act = jax.ShapeDtypeStruct((b, pairs, s // LANES, 2 * LANES, LANES), BF16)
    return pl.pallas_call(
        functools.partial(_inproj_kernel, ts=ts, half=half, sb_scale=sb_scale, ml_scale=ml_scale),
        grid=(b, s // ts),
        in_specs=[
            pl.BlockSpec((None, ts, d), lambda bi, i: (bi, i, 0)),
            _const_spec((1, d)),
            _const_spec(w_t.shape),
            _const_spec(conv_w.shape),
            _const_spec((1, 2 * half)),
        ],
        out_specs=[row, pair_blocks, pair_blocks, row, row, row, row,
                   pl.BlockSpec((None, half, ts), lambda bi, i: (bi, 0, i)),
                   row, row,
                   pl.BlockSpec((None, n_gates, ts), lambda bi, i: (bi, 0, i))],
        out_shape=[act, pair_act, pair_act, act, act, act, act,
                   jax.ShapeDtypeStruct((b, half, s), BF16),
                   act, act,
                   jax.ShapeDtypeStruct((b, n_gates, s), F32)],
        scratch_shapes=[pltpu.VMEM((SUBLANES, 2 * half), F32), pltpu.VMEM((ts, d), BF16)],
        compiler_params=pltpu.CompilerParams(
            dimension_semantics=("arbitrary", "arbitrary"), vmem_limit_bytes=VMEM_LIMIT_BYTES),
        name="inproj",
    )(x, pre_norm_w.reshape(1, d), w_t, conv_w, conv_b.reshape(1, 2 * half))


def _sb_kernel(q_ref, kc_ref, vc_ref, tt_ref, y_ref, carry_ref, exit_min_ref, *, seq, pairs, unroll):
    nb = seq // LANES
    y_ref[...] = jnp.zeros_like(y_ref)
    carry_ref[...] = jnp.zeros_like(carry_ref)
    visible = (lax.broadcasted_iota(jnp.int32, (SB_WINDOW_ROWS, LANES), 1)
               < lax.broadcasted_iota(jnp.int32, (SB_WINDOW_ROWS, LANES), 0))

    def block(kb, row0, n_rows, diagonal):
        rows = pl.ds(pl.multiple_of(row0, SB_THIN_ROWS), n_rows)

        def mask_diag(x):
            return jnp.where(visible[:n_rows], x, 0.0) if diagonal else x

        for p in range(pairs):
            lanes = slice(p * LANES, (p + 1) * LANES)
            z = _dot_nt(q_ref[rows, lanes], kc_ref[p, kb])
            a_parts = []
            for hh in range(2):
                zh = z[:, hh * LANES:(hh + 1) * LANES]
                hi, lo = _split_bf16(mask_diag(_softplus2(zh)))
                ct = _dot(jnp.concatenate([hi, lo], axis=1), tt_ref[...])
                carry = carry_ref[2 * p + hh, rows, :]
                a = mask_diag(jnp.exp2(zh - (ct[:, :LANES] + carry)))
                carry_ref[2 * p + hh, rows, :] = carry + ct[:, LANES:]
                a_parts.append(a.astype(BF16))
            y_ref[rows, lanes] += _dot(jnp.concatenate(a_parts, axis=1), vc_ref[p, kb])

    def min_carry(sub):
        rows = pl.ds(pl.multiple_of(sub * LANES, LANES), LANES)
        m = carry_ref[0, rows, :]
        for hh in range(1, 2 * pairs):
            m = jnp.minimum(m, carry_ref[hh, rows, :])
        return jnp.min(m)

    block(nb - 1, (nb - 1) * LANES, LANES, True)
    block(nb - 2, (nb - 2) * LANES, 2 * LANES, True)

    def sweep(j, carry):
        for step in range(unroll):
            kb = nb - 3 - (j * unroll + step)
            block(kb, kb * LANES, SB_WINDOW_ROWS, True)
            exit_min_ref[kb + 2] = min_carry(kb + 2)
        return carry

    lax.fori_loop(0, (nb - 2) // unroll, sweep, 0)

    def finish(sub, carry):
        @pl.when(exit_min_ref[sub] < SB_DEAD_CARRY)
        def _():
            block(sub - 2, sub * LANES + SB_THIN_ROWS, LANES - SB_THIN_ROWS, False)

            def body(state):
                kb, _ = state
                block(kb, sub * LANES, LANES, False)
                return kb - 1, min_carry(sub)

            def live(state):
                kb, m = state
                return jnp.logical_and(kb >= 0, m < SB_DEAD_CARRY)

            lax.while_loop(live, body, (sub - 3, min_carry(sub)))
        return carry

    lax.fori_loop(2, nb, finish, 0)


def _sb_attention(q, kc, vc, *, pairs_per_step):
    b, s, width = q.shape
    pairs = width // LANES
    nb = s // LANES
    assert pairs % pairs_per_step == 0 and nb >= 4
    unroll = max(u for u in range(1, SB_SWEEP_UNROLL + 1) if (nb - 2) % u == 0)
    pw = pairs_per_step * LANES
    j = lax.broadcasted_iota(jnp.int32, (LANES, LANES), 0)
    t = lax.broadcasted_iota(jnp.int32, (LANES, LANES), 1)
    tri_ones = jnp.concatenate([(j >= t).astype(BF16), jnp.ones((LANES, LANES), BF16)], axis=1)
    tt = jnp.concatenate([tri_ones, tri_ones], axis=0)
    qspec = pl.BlockSpec((None, s, pw), lambda bi, p: (bi, 0, p))
    kvspec = pl.BlockSpec((None, pairs_per_step, nb, 2 * LANES, LANES), lambda bi, p: (bi, p, 0, 0, 0))
    return pl.pallas_call(
        functools.partial(_sb_kernel, seq=s, pairs=pairs_per_step, unroll=unroll),
        grid=(b, pairs // pairs_per_step),
        in_specs=[qspec, kvspec, kvspec, _const_spec(tt.shape)],
        out_specs=qspec,
        out_shape=jax.ShapeDtypeStruct((b, s, width), F32),
        scratch_shapes=[pltpu.VMEM((2 * pairs_per_step, s, LANES), F32),
                        pltpu.SMEM((nb,), F32)],
        compiler_params=pltpu.CompilerParams(
            dimension_semantics=("arbitrary", "arbitrary"), vmem_limit_bytes=VMEM_LIMIT_BYTES),
        name="sb_attention",
    )(q, kc, vc, tt)


def _mlstm_kernel(q_ref, k_ref, v_ref, vt_ref, g_ref, gb_ref, cs_ref, h_ref, st_ref, m_ref,
                  *, ts, chunk, d, nh):
    @pl.when(pl.program_id(1) == 0)
    def _():
        st_ref[...] = jnp.zeros_like(st_ref)
        m_ref[...] = jnp.zeros_like(m_ref)

    r_i = lax.broadcasted_iota(jnp.int32, (chunk, chunk), 0)
    c_i = lax.broadcasted_iota(jnp.int32, (chunk, chunk), 1)
    lower = c_i <= r_i
    ones_v = jnp.ones((chunk, d), BF16)

    states = [st_ref[h] for h in range(nh)]
    m_run = [m_ref[h, 0:1, :] for h in range(nh)]
    for c in range(ts // chunk):
        sl = slice(c * chunk, (c + 1) * chunk)
        gates = g_ref[:, sl] + gb_ref[...]
        lf_all = _log_sigmoid(gates)
        hi, lo = _split_bf16(lf_all)
        bcum = _dot(jnp.concatenate([hi, lo], axis=1), cs_ref[...])
        for h in range(nh):
            hs = slice(h * d, (h + 1) * d)
            q = q_ref[sl, hs]
            k = k_ref[sl, hs]
            v = v_ref[sl, hs]
            vt = vt_ref[hs, sl]
            ig = gates[h:h + 1, :]
            lf = lf_all[nh + h:nh + h + 1, :]
            b = bcum[nh + h:nh + h + 1, :]
            u = ig - b
            b_last = b[:, chunk - 1:chunk]
            m_loc = jnp.max(b_last + u, axis=-1, keepdims=True)
            m_prev = m_run[h]
            state = states[h]

            b_col = jnp.sum(jnp.where(lower, lf, 0.0), axis=-1, keepdims=True)
            w = jnp.where(lower, u, -jnp.inf)
            m_row = jnp.maximum(jnp.max(w, axis=-1, keepdims=True), m_prev[:, 0:1])
            m_rep = jnp.broadcast_to(m_row, (chunk, d))
            scores = (_dot_nt(q, k) * jnp.exp(w - m_rep)).astype(BF16)
            intra = _dot(scores, jnp.concatenate([v, ones_v], axis=1))
            inter = _dot_nt(q, state.astype(BF16))
            carry_scale = jnp.exp(m_prev - m_rep)
            num = intra[:, :d] + carry_scale * inter[:, :d]
            den = intra[:, d:] + carry_scale * inter[:, d:]
            h_ref[sl, hs] = num / jnp.maximum(jnp.abs(den), jnp.exp(-(b_col + m_rep)))

            w_loc = jnp.exp(b_last + u - m_loc)
            lhs = jnp.concatenate([vt.astype(F32) * w_loc, jnp.broadcast_to(w_loc, (d, chunk))], axis=0)
            upd = _dot(lhs.astype(BF16), k)
            m_new = jnp.maximum(b_last + m_prev, m_loc)
            states[h] = jnp.exp(b_last + m_prev - m_new) * state + jnp.exp(m_loc - m_new) * upd
            m_run[h] = m_new
    for h in range(nh):
        st_ref[h] = states[h]
        m_ref[h, 0:1, :] = m_run[h]


def _mlstm(q, k, v, vt, gates, gate_bias, *, ts, chunk=LANES):
    b, s, width = q.shape
    nh = ML_HEADS
    d = width // nh
    assert d == LANES and chunk == LANES
    j = lax.broadcasted_iota(jnp.int32, (chunk, chunk), 0)
    t = lax.broadcasted_iota(jnp.int32, (chunk, chunk), 1)
    triu = (j <= t).astype(BF16)
    cs = jnp.concatenate([triu, triu], axis=0)
    row = pl.BlockSpec((None, ts, width), lambda bi, i: (bi, i, 0))
    return pl.pallas_call(
        functools.partial(_mlstm_kernel, ts=ts, chunk=chunk, d=d, nh=nh),
        grid=(b, s // ts),
        in_specs=[row, row, row,
                  pl.BlockSpec((None, width, ts), lambda bi, i: (bi, 0, i)),
                  pl.BlockSpec((None, 2 * nh, ts), lambda bi, i: (bi, 0, i)),
                  _const_spec((2 * nh, 1)),
                  _const_spec(cs.shape)],
        out_specs=row,
        out_shape=jax.ShapeDtypeStruct((b, s, width), F32),
        scratch_shapes=[pltpu.VMEM((nh, 2 * d, d), F32), pltpu.VMEM((nh, SUBLANES, LANES), F32)],
        compiler_params=pltpu.CompilerParams(
            dimension_semantics=("arbitrary", "arbitrary"), vmem_limit_bytes=VMEM_LIMIT_BYTES),
        name="mlstm",
    )(q, k, v, vt, gates, gate_bias, cs)


def _out_kernel(ysb_ref, gz_ref, hml_ref, go_ref, gmz_ref, x_ref, p_ref,
                sbw_ref, mlw_ref, gsb_ref, gml_ref, wout_ref, pw_ref, wgate_ref, bgate_ref, wup_ref,
                o_ref, *, half):
    def head_mean_square(y, g_ref):
        yy = (y * y).astype(BF16)
        return jnp.concatenate([_dot(yy[:, c:c + MXU_WIDTH], g_ref[...]) for c in range(0, half, MXU_WIDTH)], axis=1)

    ysb = ysb_ref[...]
    a = (ysb * lax.rsqrt(head_mean_square(ysb, gsb_ref) + EPS)) * sbw_ref[...] * gz_ref[...].astype(F32)
    yml = go_ref[...].astype(F32) * hml_ref[...]
    bb = (yml * lax.rsqrt(head_mean_square(yml, gml_ref) + EPS)) * mlw_ref[...] * gmz_ref[...].astype(F32)
    y = _dot(a.astype(BF16), wout_ref[0:half, :]) + _dot(bb.astype(BF16), wout_ref[half:2 * half, :])
    ms_y = jnp.mean(y * y, axis=-1, keepdims=True)
    h = x_ref[...] + (y * lax.rsqrt(ms_y + EPS)) * pw_ref[...]
    gate = _sigmoid(_dot(h.astype(BF16), wgate_ref[...]) + bgate_ref[...])
    o_ref[...] = h + gate * _dot(p_ref[...].astype(BF16), wup_ref[...])


def _head_mean_matrix(hd):
    assert MXU_WIDTH % hd == 0
    j = lax.broadcasted_iota(jnp.int32, (MXU_WIDTH, MXU_WIDTH), 0) // hd
    t = lax.broadcasted_iota(jnp.int32, (MXU_WIDTH, MXU_WIDTH), 1) // hd
    return jnp.where(j == t, 1.0 / hd, 0.0).astype(BF16)


def _out_stage(ysb, gz, hml, go, gmz, x, p, sb_norm_w, ml_norm_w, w_out, post_norm_w,
               w_gate, b_gate, w_up, *, ts):
    t, d = x.shape
    half = d // 2
    pd = p.shape[-1]
    rows = lambda w: pl.BlockSpec((ts, w), lambda i: (i, 0))
    return pl.pallas_call(
        functools.partial(_out_kernel, half=half),
        grid=(t // ts,),
        in_specs=[rows(half), rows(half), rows(half), rows(half), rows(half), rows(d), rows(pd),
                  _const_spec((1, half)), _const_spec((1, half)),
                  _const_spec((MXU_WIDTH, MXU_WIDTH)), _const_spec((MXU_WIDTH, MXU_WIDTH)),
                  _const_spec((d, d)), _const_spec((1, d)),
                  _const_spec((d, d)), _const_spec((1, d)), _const_spec((pd, d))],
        out_specs=rows(d),
        out_shape=jax.ShapeDtypeStruct((t, d), F32),
        compiler_params=pltpu.CompilerParams(
            dimension_semantics=("arbitrary",), vmem_limit_bytes=VMEM_LIMIT_BYTES),
        name="out_stage",
    )(ysb, gz, hml, go, gmz, x, p,
      sb_norm_w.reshape(1, half), ml_norm_w.reshape(1, half),
      _head_mean_matrix(half // SB_HEADS), _head_mean_matrix(half // ML_HEADS),
      w_out, post_norm_w.reshape(1, d), w_gate, b_gate.reshape(1, d), w_up)


def _layer(h, p, pre_norm_w, w_in, ml_conv_w, ml_conv_b, ml_i_bias, ml_f_bias, sb_norm_w, ml_norm_w,
           w_out, post_norm_w, ple_w_up, ple_w_gate, ple_b_gate):
    b, s, d = h.shape
    half = d // 2
    tiles = _tiles(s)
    q, kc, vc, gz, mq, mk, mv, mvt, go, gmz, gates = _inproj(
        h, pre_norm_w, w_in.T.astype(BF16), ml_conv_w, ml_conv_b, ts=tiles["inproj"])
    ysb = _sb_attention(q, kc, vc, pairs_per_step=tiles["sb_pairs"])
    gate_bias = jnp.concatenate([ml_i_bias, ml_f_bias]).reshape(2 * ML_HEADS, 1)
    hml = _mlstm(mq, mk, mv, mvt, gates, gate_bias, ts=tiles["ml"])
    flat = lambda a: a.reshape(b * s, a.shape[-1])
    out = _out_stage(flat(ysb), flat(gz), flat(hml), flat(go), flat(gmz), flat(h), flat(p),
                     sb_norm_w, ml_norm_w, w_out.astype(BF16), post_norm_w,
                     ple_w_gate.astype(BF16), ple_b_gate, ple_w_up.astype(BF16), ts=tiles["out"])
    return out.reshape(b, s, d)


def kernel(x, p, pre_norm_w, w_in, ml_conv_w, ml_conv_b, ml_i_bias, ml_f_bias, sb_norm_w, ml_norm_w,
           w_out, post_norm_w, ple_w_up, ple_w_gate, ple_b_gate):
    h = x
    for i in range(pre_norm_w.shape[0]):
        h = _layer(h, p[i], pre_norm_w[i], w_in[i], ml_conv_w[i], ml_conv_b[i], ml_i_bias[i], ml_f_bias[i],
                   sb_norm_w[i], ml_norm_w[i], w_out[i], post_norm_w[i], ple_w_up[i], ple_w_gate[i],
                   ple_b_gate[i])
    return h
```

```python
import functools

import jax
import jax.numpy as jnp
from jax import lax
from jax.experimental import pallas as pl
from jax.experimental.pallas import tpu as pltpu

SB_HEADS = 8
ML_HEADS = 4
CONV_K = 4
EPS = 1e-6
LOG2_E = 1.4426950408889634

LANES = 128
SUBLANES = 8
MXU_WIDTH = 256
VMEM_LIMIT_BYTES = 56 * 1024 * 1024

SB_DEAD_CARRY = 152.0
SB_THIN_ROWS = 32
SB_WINDOW_ROWS = 2 * LANES + SB_THIN_ROWS
SB_SWEEP_UNROLL = 10

F32 = jnp.float32
BF16 = jnp.bfloat16


def _tiles(seq):
    def pick(pref):
        t = min(pref, seq)
        assert seq % t == 0 and t % LANES == 0
        return t
    return dict(inproj=pick(1024), sb_pairs=2, ml=pick(1024), out=pick(1024))


def _dot(a, b):
    return jnp.dot(a, b, preferred_element_type=F32)


def _dot_nt(a, b):
    return lax.dot_general(a, b, (((1,), (1,)), ((), ())), preferred_element_type=F32)


def _split_bf16(x):
    hi = x.astype(BF16)
    lo = (x - hi.astype(F32)).astype(BF16)
    return hi, lo


def _sigmoid(x):
    return 0.5 + 0.5 * jnp.tanh(0.5 * x)


def _silu(x):
    h = 0.5 * x
    return h + h * jnp.tanh(h)


def _softplus(x):
    return jnp.maximum(x, 0.0) + jnp.log(1.0 + jnp.exp(-jnp.abs(x)))


def _log_sigmoid(x):
    return -_softplus(-x)


def _softplus2(x):
    return jnp.maximum(x, 0.0) + jnp.log2(1.0 + jnp.exp2(-jnp.abs(x)))


def _const_spec(shape):
    return pl.BlockSpec(shape, lambda *_: (0,) * len(shape), pipeline_mode=pl.Buffered(1))


def _store_pair_masked(dst_ref, a, ts, pair0):
    first_head = lax.broadcasted_iota(jnp.int32, (LANES, LANES), 1) < LANES // 2
    zero = jnp.zeros((LANES, LANES), a.dtype)
    for p in range(a.shape[1] // LANES):
        for j in range(ts // LANES):
            blk = a[j * LANES:(j + 1) * LANES, p * LANES:(p + 1) * LANES]
            dst_ref[pair0 + p, j, 0:LANES, :] = jnp.where(first_head, blk, zero)
            dst_ref[pair0 + p, j, LANES:2 * LANES, :] = jnp.where(first_head, zero, blk)


def _inproj_kernel(x_ref, pnw_ref, wt_ref, cw_ref, cb_ref,
                   q_ref, kc_ref, vc_ref, gz_ref, mq_ref, mk_ref, mv_ref, mvt_ref,
                   go_ref, gmz_ref, gates_ref, halo_ref, u_ref, *, ts, half, sb_scale, ml_scale):
    @pl.when(pl.program_id(1) == 0)
    def _():
        halo_ref[...] = jnp.zeros_like(halo_ref)

    x = x_ref[...]
    ms = jnp.mean(x * x, axis=-1, keepdims=True)
    u_ref[...] = ((x * lax.rsqrt(ms + EPS)) * pnw_ref[...]).astype(BF16)

    cw = MXU_WIDTH
    per_seg = half // cw

    def proj(seg, sub):
        col = seg * half + sub * cw
        return _dot_nt(u_ref[...], wt_ref[col:col + cw, :])

    def conv_silu(seg, sub):
        cols = slice((seg - 4) * half + sub * cw, (seg - 4) * half + (sub + 1) * cw)
        cur = proj(seg, sub)
        full = jnp.concatenate([halo_ref[:, cols], cur], axis=0)
        halo_ref[:, cols] = cur[ts - SUBLANES:ts]
        acc = cb_ref[:, cols]
        for j in range(CONV_K):
            delay = CONV_K - 1 - j
            tap = full if delay == 0 else pltpu.roll(full, delay, axis=0)
            acc = acc + cw_ref[j:j + 1, cols] * tap[SUBLANES:SUBLANES + ts]
        return _silu(acc)

    for sub in range(per_seg):
        out = slice(sub * cw, (sub + 1) * cw)
        mq_ref[:, out] = conv_silu(4, sub).astype(BF16)
        mk_ref[:, out] = (conv_silu(5, sub) * ml_scale).astype(BF16)
        q_ref[:, out] = (proj(0, sub) * sb_scale).astype(BF16)
        _store_pair_masked(kc_ref, proj(1, sub).astype(BF16), ts, sub * (cw // LANES))
        _store_pair_masked(vc_ref, proj(2, sub).astype(BF16), ts, sub * (cw // LANES))
        gz_ref[:, out] = _silu(proj(3, sub)).astype(BF16)
        mv = proj(6, sub)
        mv_ref[:, out] = mv.astype(BF16)
        mvt_ref[out, :] = mv.T.astype(BF16)
        go_ref[:, out] = _sigmoid(proj(7, sub)).astype(BF16)
        gmz_ref[:, out] = _silu(proj(8, sub)).astype(BF16)
    n_main = 9 * half
    gates_ref[...] = _dot_nt(wt_ref[n_main:, :], u_ref[...])


def _inproj(x, pre_norm_w, w_t, conv_w, conv_b, *, ts):
    b, s, d = x.shape
    half = d // 2
    n_gates = w_t.shape[0] - 9 * half
    sb_scale = float(half // SB_HEADS) ** -0.5 * LOG2_E
    ml_scale = float(half // ML_HEADS) ** -0.5
    row = pl.BlockSpec((None, ts, half), lambda bi, i: (bi, i, 0))
    act = jax.ShapeDtypeStruct((b, s, half), BF16)
    pairs = half // LANES
    pair_blocks = pl.BlockSpec((None, pairs, ts // LANES, 2 * LANES, LANES), lambda bi, i: (bi, 0, i, 0, 0))
    pair_act = jax.ShapeDtypeStruct((b, pairs, s // LANES, 2 * LANES, LANES), BF16)
    return pl.pallas_call(
        functools.partial(_inproj_kernel, ts=ts, half=half, sb_scale=sb_scale, ml_scale=ml_scale),
        grid=(b, s // ts),
        in_specs=[
            pl.BlockSpec((None, ts, d), lambda bi, i: (bi, i, 0)),
            _const_spec((1, d)),
            _const_spec(w_t.shape),
            _const_spec(conv_w.shape),
            _const_spec((1, 2 * half)),
        ],
        out_specs=[row, pair_blocks, pair_blocks, row, row, row, row,
                   pl.BlockSpec((None, half, ts), lambda bi, i: (bi, 0, i)),
                   row, row,
                   pl.BlockSpec((None, n_gates, ts), lambda bi, i: (bi, 0, i))],
        out_shape=[act, pair_act, pair_act, act, act, act, act,
                   jax.ShapeDtypeStruct((b, half, s), BF16),
                   act, act,
                   jax.ShapeDtypeStruct((b, n_gates, s), F32)],
        scratch_shapes=[pltpu.VMEM((SUBLANES, 2 * half), F32), pltpu.VMEM((ts, d), BF16)],
        compiler_params=pltpu.CompilerParams(
            dimension_semantics=("arbitrary", "arbitrary"), vmem_limit_bytes=VMEM_LIMIT_BYTES),
        name="inproj",
    )(x, pre_norm_w.reshape(1, d), w_t, conv_w, conv_b.reshape(1, 2 * half))


def _sb_kernel(q_ref, kc_ref, vc_ref, tt_ref, y_ref, carry_ref, exit_min_ref, *, seq, pairs, unroll):
    nb = seq // LANES
    y_ref[...] = jnp.zeros_like(y_ref)
    carry_ref[...] = jnp.zeros_like(carry_ref)
    visible = (lax.broadcasted_iota(jnp.int32, (SB_WINDOW_ROWS, LANES), 1)
               < lax.broadcasted_iota(jnp.int32, (SB_WINDOW_ROWS, LANES), 0))

    def block(kb, row0, n_rows, diagonal):
        rows = pl.ds(pl.multiple_of(row0, SB_THIN_ROWS), n_rows)

        def mask_diag(x):
            return jnp.where(visible[:n_rows], x, 0.0) if diagonal else x

        for p in range(pairs):
            lanes = slice(p * LANES, (p + 1) * LANES)
            z = _dot_nt(q_ref[rows, lanes], kc_ref[p, kb])
            a_parts = []
            for hh in range(2):
                zh = z[:, hh * LANES:(hh + 1) * LANES]
                hi, lo = _split_bf16(mask_diag(_softplus2(zh)))
                ct = _dot(jnp.concatenate([hi, lo], axis=1), tt_ref[...])
                carry = carry_ref[2 * p + hh, rows, :]
                a = mask_diag(jnp.exp2(zh - (ct[:, :LANES] + carry)))
                carry_ref[2 * p + hh, rows, :] = carry + ct[:, LANES:]
                a_parts.append(a.astype(BF16))
            y_ref[rows, lanes] += _dot(jnp.concatenate(a_parts, axis=1), vc_ref[p, kb])

    def min_carry(sub):
        rows = pl.ds(pl.multiple_of(sub * LANES, LANES), LANES)
        m = carry_ref[0, rows, :]
        for hh in range(1, 2 * pairs):
            m = jnp.minimum(m, carry_ref[hh, rows, :])
        return jnp.min(m)

    block(nb - 1, (nb - 1) * LANES, LANES, True)
    block(nb - 2, (nb - 2) * LANES, 2 * LANES, True)

    def sweep(j, carry):
        for step in range(unroll):
            kb = nb - 3 - (j * unroll + step)
            block(kb, kb * LANES, SB_WINDOW_ROWS, True)
            exit_min_ref[kb + 2] = min_carry(kb + 2)
        return carry

    lax.fori_loop(0, (nb - 2) // unroll, sweep, 0)

    def finish(sub, carry):
        @pl.when(exit_min_ref[sub] < SB_DEAD_CARRY)
        def _():
            block(sub - 2, sub * LANES + SB_THIN_ROWS, LANES - SB_THIN_ROWS, False)

            def body(state):
                kb, _ = state
                block(kb, sub * LANES, LANES, False)
                return kb - 1, min_carry(sub)

            def live(state):
                kb, m = state
                return jnp.logical_and(kb >= 0, m < SB_DEAD_CARRY)

            lax.while_loop(live, body, (sub - 3, min_carry(sub)))
        return carry

    lax.fori_loop(2, nb, finish, 0)


def _sb_attention(q, kc, vc, *, pairs_per_step):
    b, s, width = q.shape
    pairs = width // LANES
    nb = s // LANES
    assert pairs % pairs_per_step == 0 and nb >= 4
    unroll = max(u for u in range(1, SB_SWEEP_UNROLL + 1) if (nb - 2) % u == 0)
    pw = pairs_per_step * LANES
    j = lax.broadcasted_iota(jnp.int32, (LANES, LANES), 0)
    t = lax.broadcasted_iota(jnp.int32, (LANES, LANES), 1)
    tri_ones = jnp.concatenate([(j >= t).astype(BF16), jnp.ones((LANES, LANES), BF16)], axis=1)
    tt = jnp.concatenate([tri_ones, tri_ones], axis=0)
    qspec = pl.BlockSpec((None, s, pw), lambda bi, p: (bi, 0, p))
    kvspec = pl.BlockSpec((None, pairs_per_step, nb, 2 * LANES, LANES), lambda bi, p: (bi, p, 0, 0, 0))
    return pl.pallas_call(
        functools.partial(_sb_kernel, seq=s, pairs=pairs_per_step, unroll=unroll),
        grid=(b, pairs // pairs_per_step),
        in_specs=[qspec, kvspec, kvspec, _const_spec(tt.shape)],
        out_specs=qspec,
        out_shape=jax.ShapeDtypeStruct((b, s, width), F32),
        scratch_shapes=[pltpu.VMEM((2 * pairs_per_step, s, LANES), F32),
                        pltpu.SMEM((nb,), F32)],
        compiler_params=pltpu.CompilerParams(
            dimension_semantics=("arbitrary", "arbitrary"), vmem_limit_bytes=VMEM_LIMIT_BYTES),
        name="sb_attention",
    )(q, kc, vc, tt)


def _mlstm_kernel(q_ref, k_ref, v_ref, vt_ref, g_ref, gb_ref, cs_ref, h_ref, st_ref, m_ref,
                  *, ts, chunk, d, nh):
    @pl.when(pl.program_id(1) == 0)
    def _():
        st_ref[...] = jnp.zeros_like(st_ref)
        m_ref[...] = jnp.zeros_like(m_ref)

    r_i = lax.broadcasted_iota(jnp.int32, (chunk, chunk), 0)
    c_i = lax.broadcasted_iota(jnp.int32, (chunk, chunk), 1)
    lower = c_i <= r_i
    ones_v = jnp.ones((chunk, d), BF16)

    states = [st_ref[h] for h in range(nh)]
    m_run = [m_ref[h, 0:1, :] for h in range(nh)]
    for c in range(ts // chunk):
        sl = slice(c * chunk, (c + 1) * chunk)
        gates = g_ref[:, sl] + gb_ref[...]
        lf_all = _log_sigmoid(gates)
        hi, lo = _split_bf16(lf_all)
        bcum = _dot(jnp.concatenate([hi, lo], axis=1), cs_ref[...])
        for h in range(nh):
            hs = slice(h * d, (h + 1) * d)
            q = q_ref[sl, hs]
            k = k_ref[sl, hs]
            v = v_ref[sl, hs]
            vt = vt_ref[hs, sl]
            ig = gates[h:h + 1, :]
            lf = lf_all[nh + h:nh + h + 1, :]
            b = bcum[nh + h:nh + h + 1, :]
            u = ig - b
            b_last = b[:, chunk - 1:chunk]
            m_loc = jnp.max(b_last + u, axis=-1, keepdims=True)
            m_prev = m_run[h]
            state = states[h]

            b_col = jnp.sum(jnp.where(lower, lf, 0.0), axis=-1, keepdims=True)
            w = jnp.where(lower, u, -jnp.inf)
            m_row = jnp.maximum(jnp.max(w, axis=-1, keepdims=True), m_prev[:, 0:1])
            m_rep = jnp.broadcast_to(m_row, (chunk, d))
            scores = (_dot_nt(q, k) * jnp.exp(w - m_rep)).astype(BF16)
            intra = _dot(scores, jnp.concatenate([v, ones_v], axis=1))
            inter = _dot_nt(q, state.astype(BF16))
            carry_scale = jnp.exp(m_prev - m_rep)
            num = intra[:, :d] + carry_scale * inter[:, :d]
            den = intra[:, d:] + carry_scale * inter[:, d:]
            h_ref[sl, hs] = num / jnp.maximum(jnp.abs(den), jnp.exp(-(b_col + m_rep)))

            w_loc = jnp.exp(b_last + u - m_loc)
            lhs = jnp.concatenate([vt.astype(F32) * w_loc, jnp.broadcast_to(w_loc, (d, chunk))], axis=0)
            upd = _dot(lhs.astype(BF16), k)
            m_new = jnp.maximum(b_last + m_prev, m_loc)
            states[h] = jnp.exp(b_last + m_prev - m_new) * state + jnp.exp(m_loc - m_new) * upd
            m_run[h] = m_new
    for h in range(nh):
        st_ref[h] = states[h]
        m_ref[h, 0:1, :] = m_run[h]


def _mlstm(q, k, v, vt, gates, gate_bias, *, ts, chunk=LANES):
    b, s, width = q.shape
    nh = ML_HEADS
    d = width // nh
    assert d == LANES and chunk == LANES
    j = lax.broadcasted_iota(jnp.int32, (chunk, chunk), 0)
    t = lax.broadcasted_iota(jnp.int32, (chunk, chunk), 1)
    triu = (j <= t).astype(BF16)
    cs = jnp.concatenate([triu, triu], axis=0)
    row = pl.BlockSpec((None, ts, width), lambda bi, i: (bi, i, 0))
    return pl.pallas_call(
        functools.partial(_mlstm_kernel, ts=ts, chunk=chunk, d=d, nh=nh),
        grid=(b, s // ts),
        in_specs=[row, row, row,
                  pl.BlockSpec((None, width, ts), lambda bi, i: (bi, 0, i)),
                  pl.BlockSpec((None, 2 * nh, ts), lambda bi, i: (bi, 0, i)),
                  _const_spec((2 * nh, 1)),
                  _const_spec(cs.shape)],
        out_specs=row,
        out_shape=jax.ShapeDtypeStruct((b, s, width), F32),
        scratch_shapes=[pltpu.VMEM((nh, 2 * d, d), F32), pltpu.VMEM((nh, SUBLANES, LANES), F32)],
        compiler_params=pltpu.CompilerParams(
            dimension_semantics=("arbitrary", "arbitrary"), vmem_limit_bytes=VMEM_LIMIT_BYTES),
        name="mlstm",
    )(q, k, v, vt, gates, gate_bias, cs)


def _out_kernel(ysb_ref, gz_ref, hml_ref, go_ref, gmz_ref, x_ref, p_ref,
                sbw_ref, mlw_ref, wout_ref, pw_ref, wgate_ref, bgate_ref, wup_ref,
                o_ref, *, half):
    def head_mean_square(y, hd):
        first = lax.broadcasted_iota(jnp.int32, (y.shape[0], LANES), 1) < hd
        parts = []
        for c in range(0, half, LANES):
            yy = y[:, c:c + LANES] * y[:, c:c + LANES]
            total = jnp.sum(yy, axis=-1, keepdims=True)
            if hd == LANES:
                parts.append(jnp.broadcast_to(total * (1.0 / hd), yy.shape))
            else:
                left = jnp.sum(jnp.where(first, yy, 0.0), axis=-1, keepdims=True)
                parts.append(jnp.where(first, left, total - left) * (1.0 / hd))
        return jnp.concatenate(parts, axis=1)

    ysb = ysb_ref[...]
    a = ((ysb * lax.rsqrt(head_mean_square(ysb, half // SB_HEADS) + EPS)) * sbw_ref[...]
         * gz_ref[...].astype(F32))
    yml = go_ref[...].astype(F32) * hml_ref[...]
    bb = ((yml * lax.rsqrt(head_mean_square(yml, half // ML_HEADS) + EPS)) * mlw_ref[...]
          * gmz_ref[...].astype(F32))
    y = _dot(a.astype(BF16), wout_ref[0:half, :]) + _dot(bb.astype(BF16), wout_ref[half:2 * half, :])
    ms_y = jnp.mean(y * y, axis=-1, keepdims=True)
    h = x_ref[...] + (y * lax.rsqrt(ms_y + EPS)) * pw_ref[...]
    gate = _sigmoid(_dot(h.astype(BF16), wgate_ref[...]) + bgate_ref[...])
    o_ref[...] = h + gate * _dot(p_ref[...].astype(BF16), wup_ref[...])


def _out_stage(ysb, gz, hml, go, gmz, x, p, sb_norm_w, ml_norm_w, w_out, post_norm_w,
               w_gate, b_gate, w_up, *, ts):
    t, d = x.shape
    half = d // 2
    pd = p.shape[-1]
    rows = lambda w: pl.BlockSpec((ts, w), lambda i: (i, 0))
    return pl.pallas_call(
        functools.partial(_out_kernel, half=half),
        grid=(t // ts,),
        in_specs=[rows(half), rows(half), rows(half), rows(half), rows(half), rows(d), rows(pd),
                  _const_spec((1, half)), _const_spec((1, half)),
                  _const_spec((d, d)), _const_spec((1, d)),
                  _const_spec((d, d)), _const_spec((1, d)), _const_spec((pd, d))],
        out_specs=rows(d),
        out_shape=jax.ShapeDtypeStruct((t, d), F32),
        compiler_params=pltpu.CompilerParams(
            dimension_semantics=("arbitrary",), vmem_limit_bytes=VMEM_LIMIT_BYTES),
        name="out_stage",
    )(ysb, gz, hml, go, gmz, x, p,
      sb_norm_w.reshape(1, half), ml_norm_w.reshape(1, half),
      w_out, post_norm_w.reshape(1, d), w_gate, b_gate.reshape(1, d), w_up)


def _layer(h, p, pre_norm_w, w_in, ml_conv_w, ml_conv_b, ml_i_bias, ml_f_bias, sb_norm_w, ml_norm_w,
           w_out, post_norm_w, ple_w_up, ple_w_gate, ple_b_gate):
    b, s, d = h.shape
    half = d // 2
    tiles = _tiles(s)
    q, kc, vc, gz, mq, mk, mv, mvt, go, gmz, gates = _inproj(
        h, pre_norm_w, w_in.T.astype(BF16), ml_conv_w, ml_conv_b, ts=tiles["inproj"])
    ysb = _sb_attention(q, kc, vc, pairs_per_step=tiles["sb_pairs"])
    gate_bias = jnp.concatenate([ml_i_bias, ml_f_bias]).reshape(2 * ML_HEADS, 1)
    hml = _mlstm(mq, mk, mv, mvt, gates, gate_bias, ts=tiles["ml"])
    flat = lambda a: a.reshape(b * s, a.shape[-1])
    out = _out_stage(flat(ysb), flat(gz), flat(hml), flat(go), flat(gmz), flat(h), flat(p),
                     sb_norm_w, ml_norm_w, w_out.astype(BF16), post_norm_w,
                     ple_w_gate.astype(BF16), ple_b_gate, ple_w_up.astype(BF16), ts=tiles["out"])
    return out.reshape(b, s, d)


def kernel(x, p, pre_norm_w, w_in, ml_conv_w, ml_conv_b, ml_i_bias, ml_f_bias, sb_norm_w, ml_norm_w,
           w_out, post_norm_w, ple_w_up, ple_w_gate, ple_b_gate):
    h = x
    for i in range(pre_norm_w.shape[0]):
        h = _layer(h, p[i], pre_norm_w[i], w_in[i], ml_conv_w[i], ml_conv_b[i], ml_i_bias[i], ml_f_bias[i],
                   sb_norm_w[i], ml_norm_w[i], w_out[i], post_norm_w[i], ple_w_up[i], ple_w_gate[i],
                   ple_b_gate[i])
    return h
```
